```python
import jax, jax.numpy as jnp
from jax import lax
import numpy as np

D_MODEL = 2048
BATCH = 4
SEQ = 2048
DEPTH = 4
DEC_BATCH = 128
DEC_SEQ = 4
PAST_LEN = 16384
PAGE_SIZE = 128

D_MIX = D_MODEL
D_CONV = D_MIX // 2
N_CONV_GROUPS = 8
CONV_GROUP = D_CONV // N_CONV_GROUPS
CONV_A_W = 3
D_DN = D_MIX - D_CONV
N_DN_HEADS = 8
DK = D_DN // N_DN_HEADS
DV = DK
CONV_B_W = 4
CHUNK = 64
D_FF = 4 * D_MODEL
EPS = 1e-6
SPLITS = [D_CONV, 2 * D_CONV, 3 * D_CONV, 3 * D_CONV + 3 * D_DN,
          3 * D_CONV + 4 * D_DN, 3 * D_CONV + 4 * D_DN + N_DN_HEADS]
D_IN = 3 * D_CONV + 4 * D_DN + 2 * N_DN_HEADS

kernel_name = "hymba_conv_gdn_hybrid_step"


def rms_norm(x, w):
    xf = x.astype(jnp.float32)
    y = xf * lax.rsqrt(jnp.mean(xf * xf, axis=-1, keepdims=True) + EPS)
    return (y * w.astype(jnp.float32)).astype(x.dtype)


def causal_dwconv(inp, buf, w):
    width = w.shape[0]
    L = inp.shape[1]
    xp = jnp.concatenate([buf.astype(inp.dtype), inp], axis=1)
    out = sum(xp[:, i:i + L] * w[i] for i in range(width))
    return out, xp[:, -(width - 1):]


def gated_delta_chunked(q, k, v, g, beta, s0):
    bsz, L, H, _ = q.shape
    C = CHUNK if L % CHUNK == 0 else L
    N = L // C

    def blk(t):
        t = t.reshape((bsz, N, C, H) + t.shape[3:])
        return jnp.moveaxis(t, 3, 1)

    qc, kc, vc, gc, bc = blk(q), blk(k), blk(v), blk(g), blk(beta)
    G = jnp.cumsum(gc, axis=-1)
    incl = jnp.tril(jnp.ones((C, C), bool))
    decay = jnp.exp(jnp.where(incl, G[..., :, None] - G[..., None, :], -jnp.inf))
    strict = jnp.tril(jnp.ones((C, C), jnp.float32), -1)
    kk = jnp.einsum('bhncd,bhnmd->bhncm', kc, kc)
    a_mat = jnp.eye(C, dtype=jnp.float32) + bc[..., :, None] * decay * kk * strict
    rhs = jnp.concatenate([bc[..., None] * vc, (bc * jnp.exp(G))[..., None] * kc], axis=-1)
    sol = lax.linalg.triangular_solve(a_mat, rhs, left_side=True, lower=True, unit_diagonal=True)
    u_base, w_mat = sol[..., :DV], sol[..., DV:]
    p_mat = jnp.einsum('bhncd,bhnmd->bhncm', qc, kc) * decay
    q_dec = qc * jnp.exp(G)[..., None]
    k_end = kc * jnp.exp(G[..., -1:] - G)[..., None]
    gam_end = jnp.exp(G[..., -1])
    xs = tuple(jnp.moveaxis(t, 2, 0) for t in (u_base, w_mat, p_mat, q_dec, k_end, gam_end))

    def step(s, xn):
        ub, wm, pm, qd, ke, ge = xn
        u = ub - jnp.einsum('bhck,bhvk->bhcv', wm, s)
        o = jnp.einsum('bhck,bhvk->bhcv', qd, s) + jnp.einsum('bhcm,bhmv->bhcv', pm, u)
        s = ge[..., None, None] * s + jnp.einsum('bhcv,bhck->bhvk', u, ke)
        return s, o

    s_fin, o = lax.scan(step, s0, xs)
    o = jnp.transpose(o, (1, 0, 3, 2, 4)).reshape(bsz, L, H, DV)
    return o, s_fin


def mixer(h, buf_a, buf_qkv, s0, w_in, conv_a_w, conv_a_norm_w, conv_qkv_w,
          a_log, dt_bias, dn_norm_w, w_out):
    bsz, L, _ = h.shape
    proj = h @ w_in
    b_a, c_a, h_a, qkv, z, a_in, b_in = jnp.split(proj, SPLITS, axis=-1)
    conv_out, new_buf_a = causal_dwconv(c_a * h_a, buf_a, conv_a_w)
    y_a = (b_a * conv_out).reshape(bsz, L, N_CONV_GROUPS, CONV_GROUP)
    y_a = rms_norm(y_a, conv_a_norm_w.reshape(N_CONV_GROUPS, CONV_GROUP)).reshape(bsz, L, D_CONV)
    qkv_c, new_buf_qkv = causal_dwconv(qkv, buf_qkv, conv_qkv_w)
    qkv_c = jax.nn.silu(qkv_c.astype(jnp.float32)).reshape(bsz, L, 3, N_DN_HEADS, DK)
    q, k, v = qkv_c[:, :, 0], qkv_c[:, :, 1], qkv_c[:, :, 2]
    q = q * lax.rsqrt(jnp.sum(q * q, -1, keepdims=True) + EPS) * (DK ** -0.5)
    k = k * lax.rsqrt(jnp.sum(k * k, -1, keepdims=True) + EPS)
    g = -jnp.exp(a_log.astype(jnp.float32)) * jax.nn.softplus(a_in.astype(jnp.float32) + dt_bias.astype(jnp.float32))
    beta = jax.nn.sigmoid(b_in.astype(jnp.float32))
    o, s_new = gated_delta_chunked(q, k, v, g, beta, s0.astype(jnp.float32))
    zf = jax.nn.silu(z.astype(jnp.float32)).reshape(bsz, L, N_DN_HEADS, DV)
    o = (rms_norm(o, dn_norm_w) * zf).reshape(bsz, L, D_DN).astype(h.dtype)
    out = jnp.concatenate([y_a, o], axis=-1) @ w_out
    return out, new_buf_a, new_buf_qkv, s_new.astype(s0.dtype)


def run_trunk(x, bufs_a, bufs_qkv, states, norm_mix_w, w_in, conv_a_w, conv_a_norm_w,
              conv_qkv_w, a_log, dt_bias, dn_norm_w, w_out, norm_ffn_w, w_up, w_down, final_norm_w):
    new_a, new_qkv, new_s = [], [], []
    for l in range(DEPTH):
        h = rms_norm(x, norm_mix_w[l])
        m, ba, bq, s = mixer(h, bufs_a[l], bufs_qkv[l], states[l], w_in[l], conv_a_w[l],
                             conv_a_norm_w[l], conv_qkv_w[l], a_log[l], dt_bias[l],
                             dn_norm_w[l], w_out[l])
        x = x + m
        h = rms_norm(x, norm_ffn_w[l])
        x = x + jnp.square(jax.nn.relu(h @ w_up[l])) @ w_down[l]
        new_a.append(ba)
        new_qkv.append(bq)
        new_s.append(s)
    return rms_norm(x, final_norm_w), jnp.stack(new_a), jnp.stack(new_qkv), jnp.stack(new_s)


def setup_inputs(seed: int = 0) -> dict:
    key = jax.random.key(seed)
    ks = jax.random.split(key, 20)
    f32 = jnp.float32
    nrm = lambda k, s, sc: jax.random.normal(k, s, f32) * sc
    dt = jnp.exp(jax.random.uniform(ks[11], (DEPTH, N_DN_HEADS), f32, np.log(1e-3), np.log(1e-1)))
    return {
        "x_prompt": nrm(ks[0], (BATCH, SEQ, D_MODEL), 1.0),
        "x_sample": nrm(ks[1], (DEC_BATCH, DEC_SEQ, D_MODEL), 1.0),
        "state_conv_a": nrm(ks[2], (DEPTH, DEC_BATCH, CONV_A_W - 1, D_CONV), 1.0),
        "state_conv_qkv": nrm(ks[3], (DEPTH, DEC_BATCH, CONV_B_W - 1, 3 * D_DN), 1.0),
        "state_delta": nrm(ks[4], (DEPTH, DEC_BATCH, N_DN_HEADS, DV, DK), 0.05),
        "norm_mix_w": 1.0 + nrm(ks[5], (DEPTH, D_MODEL), 0.02),
        "w_in": nrm(ks[6], (DEPTH, D_MODEL, D_IN), D_MODEL ** -0.5),
        "conv_a_w": nrm(ks[7], (DEPTH, CONV_A_W, D_CONV), CONV_A_W ** -0.5),
        "conv_a_norm_w": 1.0 + nrm(ks[8], (DEPTH, D_CONV), 0.02),
        "conv_qkv_w": nrm(ks[9], (DEPTH, CONV_B_W, 3 * D_DN), CONV_B_W ** -0.5),
        "a_log": jnp.log(jax.random.uniform(ks[10], (DEPTH, N_DN_HEADS), f32, 1.0, 16.0)),
        "dt_bias": dt + jnp.log(-jnp.expm1(-dt)),
        "dn_norm_w": 1.0 + nrm(ks[12], (DEPTH, DV), 0.02),
        "w_out": nrm(ks[13], (DEPTH, D_MIX, D_MODEL), D_MIX ** -0.5),
        "norm_ffn_w": 1.0 + nrm(ks[14], (DEPTH, D_MODEL), 0.02),
        "w_up": nrm(ks[15], (DEPTH, D_MODEL, D_FF), D_MODEL ** -0.5),
        "w_down": nrm(ks[16], (DEPTH, D_FF, D_MODEL), D_FF ** -0.5),
        "final_norm_w": 1.0 + nrm(ks[17], (D_MODEL,), 0.02),
    }


def reference(x_prompt, x_sample, state_conv_a, state_conv_qkv, state_delta, norm_mix_w, w_in,
              conv_a_w, conv_a_norm_w, conv_qkv_w, a_log, dt_bias, dn_norm_w, w_out,
              norm_ffn_w, w_up, w_down, final_norm_w):
    params = (norm_mix_w, w_in, conv_a_w, conv_a_norm_w, conv_qkv_w, a_log, dt_bias,
              dn_norm_w, w_out, norm_ffn_w, w_up, w_down, final_norm_w)
    zero_a = jnp.zeros((DEPTH, BATCH, CONV_A_W - 1, D_CONV), x_prompt.dtype)
    zero_qkv = jnp.zeros((DEPTH, BATCH, CONV_B_W - 1, 3 * D_DN), x_prompt.dtype)
    zero_s = jnp.zeros((DEPTH, BATCH, N_DN_HEADS, DV, DK), state_delta.dtype)
    y_prompt, new_conv_a_p, new_conv_qkv_p, new_delta_p = run_trunk(
        x_prompt, zero_a, zero_qkv, zero_s, *params)
    y_sample, new_conv_a_s, new_conv_qkv_s, new_delta_s = run_trunk(
        x_sample, state_conv_a, state_conv_qkv, state_delta, *params)
    return (y_prompt, y_sample, new_conv_a_p, new_conv_qkv_p, new_delta_p,
            new_conv_a_s, new_conv_qkv_s, new_delta_s)
```

```python
import functools

import jax
import jax.numpy as jnp
from jax import lax
from jax.experimental import pallas as pl
from jax.experimental.pallas import tpu as pltpu

F32 = jnp.float32
BF16 = jnp.bfloat16

D_MODEL = 2048
D_CONV = 1024
D_DN = 1024
N_HEADS = 8
DK = 128
D_FF = 8192
D_MAIN = 3 * D_CONV + 4 * D_DN
QKV_OFF = 3 * D_CONV
Z_OFF = QKV_OFF + 3 * D_DN
EPS = 1e-6
LANES = 128
SUBLANES = 8
CHUNK = 128
SAMPLE_LEN = 4
VMEM_LIMIT = 56 * 1024 * 1024


def _dot(a, b):
    return jnp.dot(a, b, preferred_element_type=F32)


def _dot_nt(a, b):
    return lax.dot_general(a, b, (((1,), (1,)), ((), ())), preferred_element_type=F32)


def _softplus(x):
    return jnp.maximum(x, 0.0) + jnp.log1p(jnp.exp(-jnp.abs(x)))


def _silu(x):
    return x * jax.nn.sigmoid(x)


def _rms(x):
    return x * lax.rsqrt(jnp.mean(x * x, axis=-1, keepdims=True) + EPS)


def _proj_kernel(x_ref, nw_ref, w_ref, wab_ref, alog_ref, dtb_ref, proj_ref, gb_ref, h_scr, *, tm):
    @pl.when(pl.program_id(1) == 0)
    def _():
        for r in range(0, tm, 256):
            rows = slice(r, r + 256)
            h = (_rms(x_ref[rows, :]) * nw_ref[...]).astype(BF16)
            h_scr[rows, :] = h
            ab = _dot(h, wab_ref[...])
            lane = lax.broadcasted_iota(jnp.int32, ab.shape, 1)
            g = -jnp.exp(alog_ref[...]) * _softplus(ab + dtb_ref[...])
            gb_ref[rows, :] = jnp.where(lane < N_HEADS, g, jax.nn.sigmoid(ab))

    proj_ref[...] = _dot(h_scr[...], w_ref[...])


def _proj(x, nw, w, wab, alog, dtb, *, tm, tn=512):
    m = x.shape[0]
    return pl.pallas_call(
        functools.partial(_proj_kernel, tm=tm),
        out_shape=(jax.ShapeDtypeStruct((m, D_MAIN), F32), jax.ShapeDtypeStruct((m, LANES), F32)),
        grid=(m // tm, D_MAIN // tn),
        in_specs=[
            pl.BlockSpec((tm, D_MODEL), lambda i, j: (i, 0)),
            pl.BlockSpec((1, D_MODEL), lambda i, j: (0, 0)),
            pl.BlockSpec((D_MODEL, tn), lambda i, j: (0, j)),
            pl.BlockSpec((D_MODEL, LANES), lambda i, j: (0, 0)),
            pl.BlockSpec((1, LANES), lambda i, j: (0, 0)),
            pl.BlockSpec((1, LANES), lambda i, j: (0, 0)),
        ],
        out_specs=(pl.BlockSpec((tm, tn), lambda i, j: (i, j)),
                   pl.BlockSpec((tm, LANES), lambda i, j: (i, 0))),
        scratch_shapes=[pltpu.VMEM((tm, D_MODEL), BF16)],
        compiler_params=pltpu.CompilerParams(
            dimension_semantics=("arbitrary", "arbitrary"), vmem_limit_bytes=VMEM_LIMIT),
        name="norm_proj",
    )(x, nw, w, wab, alog, dtb)


def _outproj_kernel(y_ref, w_ref, x_ref, o_ref):
    o_ref[...] = x_ref[...] + _dot(y_ref[...], w_ref[...])


def _outproj(y, w, x, *, tm, tn=512):
    m = x.shape[0]
    return pl.pallas_call(
        _outproj_kernel,
        out_shape=jax.ShapeDtypeStruct((m, D_MODEL), F32),
        grid=(m // tm, D_MODEL // tn),
        in_specs=[
            pl.BlockSpec((tm, D_MODEL), lambda i, j: (i, 0)),
            pl.BlockSpec((D_MODEL, tn), lambda i, j: (0, j)),
            pl.BlockSpec((tm, tn), lambda i, j: (i, j)),
        ],
        out_specs=pl.BlockSpec((tm, tn), lambda i, j: (i, j)),
        compiler_params=pltpu.CompilerParams(
            dimension_semantics=("arbitrary", "arbitrary"), vmem_limit_bytes=VMEM_LIMIT),
        name="out_proj",
    )(y, w, x)


def _ffn_kernel(x_ref, nw_ref, wu_ref, wd_ref, fw_ref, o_ref, h_scr, *, tm, final):
    f = pl.program_id(1)

    @pl.when(f == 0)
    def _():
        for r in range(0, tm, 256):
            rows = slice(r, r + 256)
            x = x_ref[rows, :]
            h_scr[rows, :] = (_rms(x) * nw_ref[...]).astype(BF16)
            o_ref[rows, :] = x

    up = jnp.maximum(_dot(h_scr[...], wu_ref[...]), 0.0)
    o_ref[...] += _dot((up * up).astype(BF16), wd_ref[...])

    if final:
        @pl.when(f == pl.num_programs(1) - 1)
        def _():
            for r in range(0, tm, 256):
                rows = slice(r, r + 256)
                o_ref[rows, :] = _rms(o_ref[rows, :]) * fw_ref[...]


def _ffn(x, nw, wu, wd, fw, *, tm, final, tf=512):
    m = x.shape[0]
    return pl.pallas_call(
        functools.partial(_ffn_kernel, tm=tm, final=final),
        out_shape=jax.ShapeDtypeStruct((m, D_MODEL), F32),
        grid=(m // tm, D_FF // tf),
        in_specs=[
            pl.BlockSpec((tm, D_MODEL), lambda i, f: (i, 0)),
            pl.BlockSpec((1, D_MODEL), lambda i, f: (0, 0)),
            pl.BlockSpec((D_MODEL, tf), lambda i, f: (0, f)),
            pl.BlockSpec((tf, D_MODEL), lambda i, f: (f, 0)),
            pl.BlockSpec((1, D_MODEL), lambda i, f: (0, 0)),
        ],
        out_specs=pl.BlockSpec((tm, D_MODEL), lambda i, f: (i, 0)),
        scratch_shapes=[pltpu.VMEM((tm, D_MODEL), BF16)],
        compiler_params=pltpu.CompilerParams(
            dimension_semantics=("arbitrary", "arbitrary"), vmem_limit_bytes=VMEM_LIMIT),
        name="ffn",
    )(x, nw, wu, wd, fw)


def _group_masks(n, group):
    row = lax.broadcasted_iota(jnp.int32, (n, n), 0)
    col = lax.broadcasted_iota(jnp.int32, (n, n), 1)
    same = (row // group) == (col // group)
    incl = same & (row >= col)
    strict = same & (row > col)
    eye = jnp.where(row == col, 1.0, 0.0).astype(F32)
    return row, col, incl, strict, eye


def _merge_masks(row, col, base, group):
    masks = []
    b = base
    while b < group:
        masks.append(((row // (2 * b)) == (col // (2 * b))) & ((row // b) != (col // b)))
        b *= 2
    return masks


def _unit_lower_inverse(a, eye, base, base_mask, merge_masks):
    n1 = a if base_mask is None else jnp.where(base_mask, a, 0.0)
    n1b = n1.astype(BF16)
    n2 = _dot(n1b, n1b)
    n2b = n2.astype(BF16)
    x = eye - n1 + n2 - _dot(n1b, n2b)
    if base == 8:
        x = x + _dot(x.astype(BF16), _dot(n2b, n2b).astype(BF16))
    for m in merge_masks:
        xb = x.astype(BF16)
        xl = _dot(xb, jnp.where(m, a, 0.0).astype(BF16))
        x = x - _dot(xl.astype(BF16), xb)
    return x


def _chunk_prepare(q, k, v, gcol, bcol, grow, incl, strict, inverse):
    decay = jnp.exp(jnp.where(incl, gcol - grow, -jnp.inf))
    kt = k.T
    ktb = kt.astype(BF16)
    kk = _dot(k.astype(BF16), ktb)
    qk = _dot(q.astype(BF16), ktb)
    a = jnp.where(strict, bcol * decay * kk, 0.0)
    t = inverse(a)
    eg = jnp.exp(gcol)
    rhs = jnp.concatenate([bcol * v, (bcol * eg) * k], axis=1).astype(BF16)
    sol = _dot(t.astype(BF16), rhs)
    return sol[:, :DK], sol[:, DK:], q * eg, qk * decay, decay, kt


def _gated_out(o, z, dnw):
    return (_rms(o) * dnw * _silu(z)).astype(BF16)


def _conv_a_group(p_ref, gi, caw_ref, canw_ref, shift):
    cols = slice(gi * LANES, (gi + 1) * LANES)
    b_a = p_ref[:, cols]
    u = p_ref[:, D_CONV + gi * LANES:D_CONV + (gi + 1) * LANES] * p_ref[:, 2 * D_CONV + gi * LANES:2 * D_CONV + (gi + 1) * LANES]
    conv = caw_ref[2:3, cols] * u + caw_ref[1:2, cols] * shift(u, 1, cols) + caw_ref[0:1, cols] * shift(u, 2, cols)
    y = b_a * conv
    return u, (_rms(y) * canw_ref[:, cols]).astype(BF16)


def _conv_qkv_group(p_ref, gi, cqw_ref, shift):
    cols = slice(gi * LANES, (gi + 1) * LANES)
    x = p_ref[:, QKV_OFF + gi * LANES:QKV_OFF + (gi + 1) * LANES]
    conv = (cqw_ref[3:4, cols] * x + cqw_ref[2:3, cols] * shift(x, 1, cols)
            + cqw_ref[1:2, cols] * shift(x, 2, cols) + cqw_ref[0:1, cols] * shift(x, 3, cols))
    c = _silu(conv)
    if gi < 2 * N_HEADS:
        c = c * lax.rsqrt(jnp.sum(c * c, axis=-1, keepdims=True) + EPS)
        if gi < N_HEADS:
            c = c * (DK ** -0.5)
    return x, c


def _mixer_prompt_kernel(p_ref, gb_ref, caw_ref, canw_ref, cqw_ref, dnw_ref,
                         y_ref, lasta_ref, lastq_ref, sfin_ref,
                         qkv_scr, ua_carry, xq_carry, st_scr, *, rb):
    nb = pl.program_id(1)

    @pl.when(nb == 0)
    def _():
        ua_carry[...] = jnp.zeros_like(ua_carry)
        xq_carry[...] = jnp.zeros_like(xq_carry)
        st_scr[...] = jnp.zeros_like(st_scr)

    row8 = lax.broadcasted_iota(jnp.int32, (SUBLANES, LANES), 0)

    def make_shift(carry_ref):
        def shift(x, s, cols):
            rolled = pltpu.roll(x, s, 0)
            prev = pltpu.roll(carry_ref[:, cols], s, 0)
            first = jnp.where(row8 < s, prev, rolled[:SUBLANES])
            return jnp.concatenate([first, rolled[SUBLANES:]], axis=0)
        return shift

    shift_a = make_shift(ua_carry)
    for gi in range(D_CONV // LANES):
        cols = slice(gi * LANES, (gi + 1) * LANES)
        u, y = _conv_a_group(p_ref, gi, caw_ref, canw_ref, shift_a)
        ua_carry[:, cols] = u[rb - SUBLANES:]
        y_ref[:, cols] = y

    shift_q = make_shift(xq_carry)
    for gi in range(3 * D_DN // LANES):
        cols = slice(gi * LANES, (gi + 1) * LANES)
        x, c = _conv_qkv_group(p_ref, gi, cqw_ref, shift_q)
        xq_carry[:, cols] = x[rb - SUBLANES:]
        qkv_scr[:, cols] = c

    row, col, incl, strict, eye = _group_masks(CHUNK, CHUNK)
    base_mask = (row // 8) == (col // 8)
    merges = _merge_masks(row, col, 8, CHUNK)
    inverse = functools.partial(_unit_lower_inverse, eye=eye, base=8, base_mask=base_mask, merge_masks=merges)
    rowc = lax.broadcasted_iota(jnp.int32, (CHUNK, LANES), 0)

    for c in range(rb // CHUNK):
        rows = slice(c * CHUNK, (c + 1) * CHUNK)
        gb = gb_ref[rows, :]
        gcum = gb
        s = 1
        while s < CHUNK:
            gcum = gcum + jnp.where(rowc >= s, pltpu.roll(gcum, s, 0), 0.0)
            s *= 2
        gcum_t = gcum.T
        for h in range(N_HEADS):
            hc = slice(h * DK, (h + 1) * DK)
            q = qkv_scr[rows, hc]
            k = qkv_scr[rows, D_DN + h * DK:D_DN + (h + 1) * DK]
            v = qkv_scr[rows, 2 * D_DN + h * DK:2 * D_DN + (h + 1) * DK]
            gcol = gcum[:, h:h + 1]
            bcol = gb[:, N_HEADS + h:N_HEADS + h + 1]
            grow = gcum_t[h:h + 1, :]
            ub, w, qd, p, decay, kt = _chunk_prepare(q, k, v, gcol, bcol, grow, incl, strict, inverse)
            st = st_scr[h]
            ws = _dot(jnp.concatenate([w, qd], axis=0).astype(BF16), st.astype(BF16))
            u = ub - ws[:CHUNK]
            ub16 = u.astype(BF16)
            o = ws[CHUNK:] + _dot(p.astype(BF16), ub16)
            ket = kt * decay[CHUNK - 1:CHUNK, :]
            st_scr[h] = jnp.exp(gcol[CHUNK - 1:CHUNK, :]) * st + _dot(ket.astype(BF16), ub16)
            y_ref[rows, D_CONV + h * DK:D_CONV + (h + 1) * DK] = _gated_out(
                o, p_ref[rows, Z_OFF + h * DK:Z_OFF + (h + 1) * DK], dnw_ref[...])

    @pl.when(nb == pl.num_programs(1) - 1)
    def _():
        lasta_ref[0] = ua_carry[...]
        lastq_ref[0] = xq_carry[...]
        for h in range(N_HEADS):
            sfin_ref[0, h] = st_scr[h].T


def _mixer_prompt(proj, gb, caw, canw, cqw, dnw, *, n_seq, seq_len, rb=256):
    nblk = seq_len // rb
    m = n_seq * seq_len
    full = lambda shape: pl.BlockSpec(shape, lambda b, n: (0,) * len(shape))
    return pl.pallas_call(
        functools.partial(_mixer_prompt_kernel, rb=rb),
        out_shape=(
            jax.ShapeDtypeStruct((m, D_MODEL), BF16),
            jax.ShapeDtypeStruct((n_seq, SUBLANES, D_CONV), F32),
            jax.ShapeDtypeStruct((n_seq, SUBLANES, 3 * D_DN), F32),
            jax.ShapeDtypeStruct((n_seq, N_HEADS, DK, DK), F32),
        ),
        grid=(n_seq, nblk),
        in_specs=[
            pl.BlockSpec((rb, D_MAIN), lambda b, n: (b * nblk + n, 0)),
            pl.BlockSpec((rb, LANES), lambda b, n: (b * nblk + n, 0)),
            full((SUBLANES, D_CONV)),
            full((1, D_CONV)),
            full((SUBLANES, 3 * D_DN)),
            full((1, DK)),
        ],
        out_specs=(
            pl.BlockSpec((rb, D_MODEL), lambda b, n: (b * nblk + n, 0)),
            pl.BlockSpec((1, SUBLANES, D_CONV), lambda b, n: (b, 0, 0)),
            pl.BlockSpec((1, SUBLANES, 3 * D_DN), lambda b, n: (b, 0, 0)),
            pl.BlockSpec((1, N_HEADS, DK, DK), lambda b, n: (b, 0, 0, 0)),
        ),
        scratch_shapes=[
            pltpu.VMEM((rb, 3 * D_DN), F32),
            pltpu.VMEM((SUBLANES, D_CONV), F32),
            pltpu.VMEM((SUBLANES, 3 * D_DN), F32),
            pltpu.VMEM((N_HEADS, DK, DK), F32),
        ],
        compiler_params=pltpu.CompilerParams(
            dimension_semantics=("arbitrary", "arbitrary"), vmem_limit_bytes=VMEM_LIMIT),
        name="mixer_prompt",
    )(proj, gb, caw, canw, cqw, dnw)


SEQ_PER_STEP = 8
ROWS_PER_STEP = SEQ_PER_STEP * SAMPLE_LEN


def _mixer_sample_kernel(p_ref, gb_ref, bufa_ref, bufq_ref, caw_ref, canw_ref, cqw_ref, dnw_ref, s_ref,
                         y_ref, ua_ref, snew_ref,
                         w_scr, qd_scr, ub_scr, ke_scr, p_scr, u_scr, qs_scr, g_scr):
    j = pl.program_id(1)
    n = CHUNK

    @pl.when(j == 0)
    def _():
        rown = lax.broadcasted_iota(jnp.int32, (n, LANES), 0)
        tok = rown % SAMPLE_LEN

        def make_shift(buf_ref):
            def shift(x, s, cols):
                return jnp.where(tok >= s, pltpu.roll(x, s, 0),
                                 pltpu.roll(buf_ref[:, cols], n - (SAMPLE_LEN - s), 0))
            return shift

        shift_a = make_shift(bufa_ref)
        for gi in range(D_CONV // LANES):
            cols = slice(gi * LANES, (gi + 1) * LANES)
            u, y = _conv_a_group(p_ref, gi, caw_ref, canw_ref, shift_a)
            ua_ref[:, cols] = u
            y_ref[:, cols] = y

        shift_q = make_shift(bufq_ref)
        qkv = [None] * (3 * N_HEADS)
        for gi in range(3 * D_DN // LANES):
            _, qkv[gi] = _conv_qkv_group(p_ref, gi, cqw_ref, shift_q)

        row, col, incl, strict, eye = _group_masks(n, SAMPLE_LEN)
        inverse = functools.partial(_unit_lower_inverse, eye=eye, base=SAMPLE_LEN, base_mask=None, merge_masks=[])

        gb = gb_ref[...]
        gcum = gb
        s = 1
        while s < SAMPLE_LEN:
            gcum = gcum + jnp.where(tok >= s, pltpu.roll(gcum, s, 0), 0.0)
            s *= 2
        glast = gcum
        for back in range(1, SAMPLE_LEN):
            glast = jnp.where(tok == SAMPLE_LEN - 1 - back, pltpu.roll(gcum, n - back, 0), glast)
        g_scr[...] = gcum
        gcum_t = gcum.T
        e_end = jnp.exp(glast - gcum)
        for h in range(N_HEADS):
            hc = slice(h * DK, (h + 1) * DK)
            q, k, v = qkv[h], qkv[N_HEADS + h], qkv[2 * N_HEADS + h]
            gcol = gcum[:, h:h + 1]
            bcol = gb[:, N_HEADS + h:N_HEADS + h + 1]
            grow = gcum_t[h:h + 1, :]
            ub, w, qd, p, _, _ = _chunk_prepare(q, k, v, gcol, bcol, grow, incl, strict, inverse)
            ub_scr[:, hc] = ub
            w_scr[:, hc] = w
            qd_scr[:, hc] = qd
            ke_scr[:, hc] = k * e_end[:, h:h + 1]
            p_scr[h] = p
        u_scr[...] = jnp.zeros_like(u_scr)

    r0 = pl.multiple_of(j * ROWS_PER_STEP, ROWS_PER_STEP)
    row16 = lax.broadcasted_iota(jnp.int32, (2 * SUBLANES, LANES), 0)
    first_of_pair = (row16 % SUBLANES) < SAMPLE_LEN
    coln = lax.broadcasted_iota(jnp.int32, (DK, n), 1)
    for h in range(N_HEADS):
        hc = slice(h * DK, (h + 1) * DK)
        for t in range(ROWS_PER_STEP // SUBLANES):
            rows = pl.ds(pl.multiple_of(r0 + t * SUBLANES, SUBLANES), SUBLANES)
            lhs = jnp.concatenate([w_scr[rows, hc], qd_scr[rows, hc]], axis=0).astype(BF16)
            da = _dot_nt(lhs, s_ref[2 * t, h].astype(BF16))
            db = _dot_nt(lhs, s_ref[2 * t + 1, h].astype(BF16))
            ws = jnp.where(first_of_pair, da, db)
            u_scr[rows, hc] = ub_scr[rows, hc] - ws[:SUBLANES]
            qs_scr[rows, hc] = ws[SUBLANES:]
        u_h = u_scr[:, hc]
        u_t = u_h.T
        ke16 = ke_scr[:, hc].astype(BF16)
        for sq in range(SEQ_PER_STEP):
            seq = j * SEQ_PER_STEP + sq
            upd = _dot(jnp.where(coln // SAMPLE_LEN == seq, u_t, 0.0).astype(BF16), ke16)
            g_end = g_scr[pl.ds(r0 + sq * SAMPLE_LEN + SAMPLE_LEN - 1, 1), :]
            snew_ref[sq, h] = jnp.exp(g_end[:, h:h + 1]) * s_ref[sq, h] + upd
        rows = pl.ds(r0, ROWS_PER_STEP)
        o = qs_scr[rows, hc] + _dot(p_scr[h, rows, :].astype(BF16), u_h.astype(BF16))
        y_ref[rows, D_CONV + h * DK:D_CONV + (h + 1) * DK] = _gated_out(
            o, p_ref[rows, Z_OFF + h * DK:Z_OFF + (h + 1) * DK], dnw_ref[...])


def _mixer_sample(proj, gb, bufa, bufq, caw, canw, cqw, dnw, state):
    m = proj.shape[0]
    n = CHUNK
    steps = n // ROWS_PER_STEP
    full = lambda shape: pl.BlockSpec(shape, lambda i, j: (0,) * len(shape))
    rowblk = lambda width: pl.BlockSpec((n, width), lambda i, j: (i, 0))
    sblk = pl.BlockSpec((SEQ_PER_STEP, N_HEADS, DK, DK), lambda i, j: (i * steps + j, 0, 0, 0))
    return pl.pallas_call(
        _mixer_sample_kernel,
        out_shape=(
            jax.ShapeDtypeStruct((m, D_MODEL), BF16),
            jax.ShapeDtypeStruct((m, D_CONV), F32),
            jax.ShapeDtypeStruct(state.shape, F32),
        ),
        grid=(m // n, steps),
        in_specs=[
            rowblk(D_MAIN), rowblk(LANES), rowblk(D_CONV), rowblk(3 * D_DN),
            full((SUBLANES, D_CONV)), full((1, D_CONV)), full((SUBLANES, 3 * D_DN)), full((1, DK)),
            sblk,
        ],
        out_specs=(rowblk(D_MODEL), rowblk(D_CONV), sblk),
        scratch_shapes=[
            pltpu.VMEM((n, D_DN), F32),
            pltpu.VMEM((n, D_DN), F32),
            pltpu.VMEM((n, D_DN), F32),
            pltpu.VMEM((n, D_DN), F32),
            pltpu.VMEM((N_HEADS, n, n), F32),
            pltpu.VMEM((n, D_DN), F32),
            pltpu.VMEM((n, D_DN), F32),
            pltpu.VMEM((n, LANES), F32),
        ],
        compiler_params=pltpu.CompilerParams(
            dimension_semantics=("arbitrary", "arbitrary"), vmem_limit_bytes=VMEM_LIMIT),
        name="mixer_sample",
    )(proj, gb, bufa, bufq, caw, canw, cqw, dnw, state)


def _pad_rows(w):
    return jnp.pad(w, ((0, 0), (0, SUBLANES - w.shape[1]), (0, 0)))


def kernel(x_prompt, x_sample, state_conv_a, state_conv_qkv, state_delta, norm_mix_w, w_in,
           conv_a_w, conv_a_norm_w, conv_qkv_w, a_log, dt_bias, dn_norm_w, w_out,
           norm_ffn_w, w_up, w_down, final_norm_w):
    depth = w_in.shape[0]
    n_seq, seq_len, _ = x_prompt.shape
    n_dec, dec_len, _ = x_sample.shape
    assert dec_len == SAMPLE_LEN

    w_main = w_in[:, :, :D_MAIN].astype(BF16)
    w_ab = jnp.pad(w_in[:, :, D_MAIN:], ((0, 0), (0, 0), (0, LANES - 2 * N_HEADS))).astype(BF16)
    w_out16, w_up16, w_down16 = w_out.astype(BF16), w_up.astype(BF16), w_down.astype(BF16)
    alog = jnp.pad(a_log, ((0, 0), (0, LANES - N_HEADS)))[:, None, :]
    dtb = jnp.pad(dt_bias, ((0, 0), (0, LANES - N_HEADS)))[:, None, :]
    caw, cqw = _pad_rows(conv_a_w), _pad_rows(conv_qkv_w)
    bufa = jnp.pad(state_conv_a, ((0, 0), (0, 0), (SAMPLE_LEN - state_conv_a.shape[2], 0), (0, 0))
                   ).reshape(depth, n_dec * SAMPLE_LEN, D_CONV)
    bufq = jnp.pad(state_conv_qkv, ((0, 0), (0, 0), (SAMPLE_LEN - state_conv_qkv.shape[2], 0), (0, 0))
                   ).reshape(depth, n_dec * SAMPLE_LEN, 3 * D_DN)

    xp = x_prompt.reshape(n_seq * seq_len, D_MODEL)
    xs = x_sample.reshape(n_dec * dec_len, D_MODEL)
    tm_p, tm_s = 512, 512
    fw = final_norm_w[None, :]
    conv_a_p, conv_q_p, delta_p, conv_a_s, conv_q_s, delta_s = [], [], [], [], [], []
    for l in range(depth):
        nmw, nfw = norm_mix_w[l][None, :], norm_ffn_w[l][None, :]
        canw, dnw = conv_a_norm_w[l][None, :], dn_norm_w[l][None, :]
        final = l == depth - 1
        proj, gb = _proj(xp, nmw, w_main[l], w_ab[l], alog[l], dtb[l], tm=tm_p)
        y, last_a, last_q, s_fin = _mixer_prompt(proj, gb, caw[l], canw, cqw[l], dnw, n_seq=n_seq, seq_len=seq_len)
        xp = _outproj(y, w_out16[l], xp, tm=tm_p)
        xp = _ffn(xp, nfw, w_up16[l], w_down16[l], fw, tm=tm_p, final=final)
        conv_a_p.append(last_a[:, SUBLANES - 2:])
        conv_q_p.append(last_q[:, SUBLANES - 3:])
        delta_p.append(s_fin)
        proj, gb = _proj(xs, nmw, w_main[l], w_ab[l], alog[l], dtb[l], tm=tm_s)
        y, ua, s_new = _mixer_sample(proj, gb, bufa[l], bufq[l], caw[l], canw, cqw[l], dnw, state_delta[l])
        xs = _outproj(y, w_out16[l], xs, tm=tm_s)
        xs = _ffn(xs, nfw, w_up16[l], w_down16[l], fw, tm=tm_s, final=final)
        conv_a_s.append(ua.reshape(n_dec, SAMPLE_LEN, D_CONV)[:, SAMPLE_LEN - 2:])
        conv_q_s.append(proj[:, QKV_OFF:QKV_OFF + 3 * D_DN].reshape(n_dec, SAMPLE_LEN, 3 * D_DN)[:, SAMPLE_LEN - 3:])
        delta_s.append(s_new)

    return (xp.reshape(n_seq, seq_len, D_MODEL), xs.reshape(n_dec, dec_len, D_MODEL),
            jnp.stack(conv_a_p), jnp.stack(conv_q_p), jnp.stack(delta_p),
            jnp.stack(conv_a_s), jnp.stack(conv_q_s), jnp.stack(delta_s))
```

```python
import functools

import jax
import jax.numpy as jnp
from jax import lax
from jax.experimental import pallas as pl
from jax.experimental.pallas import tpu as pltpu

F32 = jnp.float32
BF16 = jnp.bfloat16

D_MODEL = 2048
D_CONV = 1024
D_DN = 1024
N_HEADS = 8
DK = 128
D_FF = 8192
D_MAIN = 3 * D_CONV + 4 * D_DN
QKV_OFF = 3 * D_CONV
Z_OFF = QKV_OFF + 3 * D_DN
EPS = 1e-6
LANES = 128
SUBLANES = 8
CHUNK = 128
SAMPLE_LEN = 4
VMEM_LIMIT = 56 * 1024 * 1024


def _dot(a, b):
    return jnp.dot(a, b, preferred_element_type=F32)


def _dot_nt(a, b):
    return lax.dot_general(a, b, (((1,), (1,)), ((), ())), preferred_element_type=F32)


def _softplus(x):
    return jnp.maximum(x, 0.0) + jnp.log1p(jnp.exp(-jnp.abs(x)))


def _silu(x):
    return x * jax.nn.sigmoid(x)


def _rms(x):
    return x * lax.rsqrt(jnp.mean(x * x, axis=-1, keepdims=True) + EPS)


def _proj_kernel(x_ref, nw_ref, w_ref, wab_ref, alog_ref, dtb_ref, proj_ref, gb_ref, h_scr, *, tm):
    @pl.when(pl.program_id(1) == 0)
    def _():
        for r in range(0, tm, 256):
            rows = slice(r, r + 256)
            h = (_rms(x_ref[rows, :]) * nw_ref[...]).astype(BF16)
            h_scr[rows, :] = h
            ab = _dot(h, wab_ref[...])
            lane = lax.broadcasted_iota(jnp.int32, ab.shape, 1)
            g = -jnp.exp(alog_ref[...]) * _softplus(ab + dtb_ref[...])
            gb_ref[rows, :] = jnp.where(lane < N_HEADS, g, jax.nn.sigmoid(ab))

    proj_ref[...] = _dot(h_scr[...], w_ref[...])


def _proj(x, nw, w, wab, alog, dtb, *, tm, tn=512):
    m = x.shape[0]
    return pl.pallas_call(
        functools.partial(_proj_kernel, tm=tm),
        out_shape=(jax.ShapeDtypeStruct((m, D_MAIN), F32), jax.ShapeDtypeStruct((m, LANES), F32)),
        grid=(m // tm, D_MAIN // tn),
        in_specs=[
            pl.BlockSpec((tm, D_MODEL), lambda i, j: (i, 0)),
            pl.BlockSpec((1, D_MODEL), lambda i, j: (0, 0)),
            pl.BlockSpec((D_MODEL, tn), lambda i, j: (0, j)),
            pl.BlockSpec((D_MODEL, LANES), lambda i, j: (0, 0)),
            pl.BlockSpec((1, LANES), lambda i, j: (0, 0)),
            pl.BlockSpec((1, LANES), lambda i, j: (0, 0)),
        ],
        out_specs=(pl.BlockSpec((tm, tn), lambda i, j: (i, j)),
                   pl.BlockSpec((tm, LANES), lambda i, j: (i, 0))),
        scratch_shapes=[pltpu.VMEM((tm, D_MODEL), BF16)],
        compiler_params=pltpu.CompilerParams(
            dimension_semantics=("arbitrary", "arbitrary"), vmem_limit_bytes=VMEM_LIMIT),
        name="norm_proj",
    )(x, nw, w, wab, alog, dtb)


def _outproj_kernel(y_ref, w_ref, x_ref, o_ref):
    o_ref[...] = x_ref[...] + _dot(y_ref[...], w_ref[...])


def _outproj(y, w, x, *, tm, tn=512):
    m = x.shape[0]
    return pl.pallas_call(
        _outproj_kernel,
        out_shape=jax.ShapeDtypeStruct((m, D_MODEL), F32),
        grid=(m // tm, D_MODEL // tn),
        in_specs=[
            pl.BlockSpec((tm, D_MODEL), lambda i, j: (i, 0)),
            pl.BlockSpec((D_MODEL, tn), lambda i, j: (0, j)),
            pl.BlockSpec((tm, tn), lambda i, j: (i, j)),
        ],
        out_specs=pl.BlockSpec((tm, tn), lambda i, j: (i, j)),
        compiler_params=pltpu.CompilerParams(
            dimension_semantics=("arbitrary", "arbitrary"), vmem_limit_bytes=VMEM_LIMIT),
        name="out_proj",
    )(y, w, x)


def _ffn_kernel(x_ref, nw_ref, wu_ref, wd_ref, fw_ref, o_ref, h_scr, *, tm, final):
    f = pl.program_id(1)

    @pl.when(f == 0)
    def _():
        for r in range(0, tm, 256):
            rows = slice(r, r + 256)
            x = x_ref[rows, :]
            h_scr[rows, :] = (_rms(x) * nw_ref[...]).astype(BF16)
            o_ref[rows, :] = x

    up = jnp.maximum(_dot(h_scr[...], wu_ref[...]), 0.0)
    o_ref[...] += _dot((up * up).astype(BF16), wd_ref[...])

    if final:
        @pl.when(f == pl.num_programs(1) - 1)
        def _():
            for r in range(0, tm, 256):
                rows = slice(r, r + 256)
                o_ref[rows, :] = _rms(o_ref[rows, :]) * fw_ref[...]


def _ffn(x, nw, wu, wd, fw, *, tm, final, tf=512):
    m = x.shape[0]
    return pl.pallas_call(
        functools.partial(_ffn_kernel, tm=tm, final=final),
        out_shape=jax.ShapeDtypeStruct((m, D_MODEL), F32),
        grid=(m // tm, D_FF // tf),
        in_specs=[
            pl.BlockSpec((tm, D_MODEL), lambda i, f: (i, 0)),
            pl.BlockSpec((1, D_MODEL), lambda i, f: (0, 0)),
            pl.BlockSpec((D_MODEL, tf), lambda i, f: (0, f)),
            pl.BlockSpec((tf, D_MODEL), lambda i, f: (f, 0)),
            pl.BlockSpec((1, D_MODEL), lambda i, f: (0, 0)),
        ],
        out_specs=pl.BlockSpec((tm, D_MODEL), lambda i, f: (i, 0)),
        scratch_shapes=[pltpu.VMEM((tm, D_MODEL), BF16)],
        compiler_params=pltpu.CompilerParams(
            dimension_semantics=("arbitrary", "arbitrary"), vmem_limit_bytes=VMEM_LIMIT),
        name="ffn",
    )(x, nw, wu, wd, fw)


def _group_masks(n, group):
    row = lax.broadcasted_iota(jnp.int32, (n, n), 0)
    col = lax.broadcasted_iota(jnp.int32, (n, n), 1)
    same = (row // group) == (col // group)
    incl = same & (row >= col)
    strict = same & (row > col)
    eye = jnp.where(row == col, 1.0, 0.0).astype(F32)
    return row, col, incl, strict, eye


def _merge_masks(row, col, base, group):
    masks = []
    b = base
    while b < group:
        masks.append(((row // (2 * b)) == (col // (2 * b))) & ((row // b) != (col // b)))
        b *= 2
    return masks


def _each(f, *lists):
    return [f(*args) for args in zip(*lists)]


def _unit_lower_inverse(a, eye, base, base_mask, merge_masks):
    n1 = a if base_mask is None else _each(lambda t: jnp.where(base_mask, t, 0.0), a)
    n1b = _each(lambda t: t.astype(BF16), n1)
    n2 = _each(_dot, n1b, n1b)
    n2b = _each(lambda t: t.astype(BF16), n2)
    n3 = _each(_dot, n1b, n2b)
    x = _each(lambda p1, p2, p3: eye - p1 + p2 - p3, n1, n2, n3)
    if base == 8:
        n4b = _each(lambda t: _dot(t, t).astype(BF16), n2b)
        x = _each(lambda t, p4: t + _dot(t.astype(BF16), p4), x, n4b)
    for m in merge_masks:
        xb = _each(lambda t: t.astype(BF16), x)
        xl = _each(lambda tb, t: _dot(tb, jnp.where(m, t, 0.0).astype(BF16)).astype(BF16), xb, a)
        x = _each(lambda t, l, tb: t - _dot(l, tb), x, xl, xb)
    return x


def _chunk_prepare(q, k, v, gcol, bcol, grow, incl, strict, inverse):
    decay = _each(lambda gc, gr: jnp.exp(jnp.where(incl, gc - gr, -jnp.inf)), gcol, grow)
    kt = _each(lambda t: t.T, k)
    ktb = _each(lambda t: t.astype(BF16), kt)
    kk = _each(lambda t, tb: _dot(t.astype(BF16), tb), k, ktb)
    qk = _each(lambda t, tb: _dot(t.astype(BF16), tb), q, ktb)
    a = _each(lambda b, d, m: jnp.where(strict, b * d * m, 0.0), bcol, decay, kk)
    t = inverse(a)
    eg = _each(jnp.exp, gcol)
    rhs = _each(lambda b, e, vv, kx: jnp.concatenate([b * vv, (b * e) * kx], axis=1).astype(BF16), bcol, eg, v, k)
    sol = _each(lambda tt, r: _dot(tt.astype(BF16), r), t, rhs)
    return (_each(lambda s: s[:, :DK], sol), _each(lambda s: s[:, DK:], sol),
            _each(lambda qq, e: qq * e, q, eg), _each(lambda m, d: m * d, qk, decay), decay, kt)


def _gated_out(o, z, dnw):
    return (_rms(o) * dnw * _silu(z)).astype(BF16)


def _conv_a_group(p_ref, gi, caw_ref, canw_ref, shift):
    cols = slice(gi * LANES, (gi + 1) * LANES)
    b_a = p_ref[:, cols]
    u = p_ref[:, D_CONV + gi * LANES:D_CONV + (gi + 1) * LANES] * p_ref[:, 2 * D_CONV + gi * LANES:2 * D_CONV + (gi + 1) * LANES]
    conv = caw_ref[2:3, cols] * u + caw_ref[1:2, cols] * shift(u, 1, cols) + caw_ref[0:1, cols] * shift(u, 2, cols)
    y = b_a * conv
    return u, (_rms(y) * canw_ref[:, cols]).astype(BF16)


def _conv_qkv_group(p_ref, gi, cqw_ref, shift):
    cols = slice(gi * LANES, (gi + 1) * LANES)
    x = p_ref[:, QKV_OFF + gi * LANES:QKV_OFF + (gi + 1) * LANES]
    conv = (cqw_ref[3:4, cols] * x + cqw_ref[2:3, cols] * shift(x, 1, cols)
            + cqw_ref[1:2, cols] * shift(x, 2, cols) + cqw_ref[0:1, cols] * shift(x, 3, cols))
    c = _silu(conv)
    if gi < 2 * N_HEADS:
        c = c * lax.rsqrt(jnp.sum(c * c, axis=-1, keepdims=True) + EPS)
        if gi < N_HEADS:
            c = c * (DK ** -0.5)
    return x, c


def _mixer_prompt_kernel(p_ref, gb_ref, caw_ref, canw_ref, cqw_ref, dnw_ref,
                         y_ref, lasta_ref, lastq_ref, sfin_ref,
                         qkv_scr, ua_carry, xq_carry, st_scr, *, rb):
    nb = pl.program_id(1)

    @pl.when(nb == 0)
    def _():
        ua_carry[...] = jnp.zeros_like(ua_carry)
        xq_carry[...] = jnp.zeros_like(xq_carry)
        st_scr[...] = jnp.zeros_like(st_scr)

    row8 = lax.broadcasted_iota(jnp.int32, (SUBLANES, LANES), 0)

    def make_shift(carry_ref):
        def shift(x, s, cols):
            rolled = pltpu.roll(x, s, 0)
            prev = pltpu.roll(carry_ref[:, cols], s, 0)
            first = jnp.where(row8 < s, prev, rolled[:SUBLANES])
            return jnp.concatenate([first, rolled[SUBLANES:]], axis=0)
        return shift

    shift_a = make_shift(ua_carry)
    for gi in range(D_CONV // LANES):
        cols = slice(gi * LANES, (gi + 1) * LANES)
        u, y = _conv_a_group(p_ref, gi, caw_ref, canw_ref, shift_a)
        ua_carry[:, cols] = u[rb - SUBLANES:]
        y_ref[:, cols] = y

    shift_q = make_shift(xq_carry)
    for gi in range(3 * D_DN // LANES):
        cols = slice(gi * LANES, (gi + 1) * LANES)
        x, c = _conv_qkv_group(p_ref, gi, cqw_ref, shift_q)
        xq_carry[:, cols] = x[rb - SUBLANES:]
        qkv_scr[:, cols] = c

    row, col, incl, strict, eye = _group_masks(CHUNK, CHUNK)
    base_mask = (row // 8) == (col // 8)
    merges = _merge_masks(row, col, 8, CHUNK)
    inverse = functools.partial(_unit_lower_inverse, eye=eye, base=8, base_mask=base_mask, merge_masks=merges)
    rowc = lax.broadcasted_iota(jnp.int32, (CHUNK, LANES), 0)

    for c in range(rb // CHUNK):
        rows = slice(c * CHUNK, (c + 1) * CHUNK)
        gb = gb_ref[rows, :]
        gcum = gb
        s = 1
        while s < CHUNK:
            gcum = gcum + jnp.where(rowc >= s, pltpu.roll(gcum, s, 0), 0.0)
            s *= 2
        gcum_t = gcum.T
        heads = list(range(N_HEADS))
        q = [qkv_scr[rows, h * DK:(h + 1) * DK] for h in heads]
        k = [qkv_scr[rows, D_DN + h * DK:D_DN + (h + 1) * DK] for h in heads]
        v = [qkv_scr[rows, 2 * D_DN + h * DK:2 * D_DN + (h + 1) * DK] for h in heads]
        gcol = [gcum[:, h:h + 1] for h in heads]
        bcol = [gb[:, N_HEADS + h:N_HEADS + h + 1] for h in heads]
        grow = [gcum_t[h:h + 1, :] for h in heads]
        ub, w, qd, p, decay, kt = _chunk_prepare(q, k, v, gcol, bcol, grow, incl, strict, inverse)
        st = [st_scr[h] for h in heads]
        ws = _each(lambda a, b, s: _dot(jnp.concatenate([a, b], axis=0).astype(BF16), s.astype(BF16)), w, qd, st)
        u16 = _each(lambda a, b: (a - b[:CHUNK]).astype(BF16), ub, ws)
        o = _each(lambda a, b, c: a[CHUNK:] + _dot(b.astype(BF16), c), ws, p, u16)
        ket = _each(lambda a, d: (a * d[CHUNK - 1:CHUNK, :]).astype(BF16), kt, decay)
        st_new = _each(lambda g, s, a, b: jnp.exp(g[CHUNK - 1:CHUNK, :]) * s + _dot(a, b), gcol, st, ket, u16)
        for h in heads:
            st_scr[h] = st_new[h]
            y_ref[rows, D_CONV + h * DK:D_CONV + (h + 1) * DK] = _gated_out(
                o[h], p_ref[rows, Z_OFF + h * DK:Z_OFF + (h + 1) * DK], dnw_ref[...])

    @pl.when(nb == pl.num_programs(1) - 1)
    def _():
        lasta_ref[0] = ua_carry[...]
        lastq_ref[0] = xq_carry[...]
        for h in range(N_HEADS):
            sfin_ref[0, h] = st_scr[h].T


def _mixer_prompt(proj, gb, caw, canw, cqw, dnw, *, n_seq, seq_len, rb=256):
    nblk = seq_len // rb
    m = n_seq * seq_len
    full = lambda shape: pl.BlockSpec(shape, lambda b, n: (0,) * len(shape))
    return pl.pallas_call(
        functools.partial(_mixer_prompt_kernel, rb=rb),
        out_shape=(
            jax.ShapeDtypeStruct((m, D_MODEL), BF16),
            jax.ShapeDtypeStruct((n_seq, SUBLANES, D_CONV), F32),
            jax.ShapeDtypeStruct((n_seq, SUBLANES, 3 * D_DN), F32),
            jax.ShapeDtypeStruct((n_seq, N_HEADS, DK, DK), F32),
        ),
        grid=(n_seq, nblk),
        in_specs=[
            pl.BlockSpec((rb, D_MAIN), lambda b, n: (b * nblk + n, 0)),
            pl.BlockSpec((rb, LANES), lambda b, n: (b * nblk + n, 0)),
            full((SUBLANES, D_CONV)),
            full((1, D_CONV)),
            full((SUBLANES, 3 * D_DN)),
            full((1, DK)),
        ],
        out_specs=(
            pl.BlockSpec((rb, D_MODEL), lambda b, n: (b * nblk + n, 0)),
            pl.BlockSpec((1, SUBLANES, D_CONV), lambda b, n: (b, 0, 0)),
            pl.BlockSpec((1, SUBLANES, 3 * D_DN), lambda b, n: (b, 0, 0)),
            pl.BlockSpec((1, N_HEADS, DK, DK), lambda b, n: (b, 0, 0, 0)),
        ),
        scratch_shapes=[
            pltpu.VMEM((rb, 3 * D_DN), F32),
            pltpu.VMEM((SUBLANES, D_CONV), F32),
            pltpu.VMEM((SUBLANES, 3 * D_DN), F32),
            pltpu.VMEM((N_HEADS, DK, DK), F32),
        ],
        compiler_params=pltpu.CompilerParams(
            dimension_semantics=("arbitrary", "arbitrary"), vmem_limit_bytes=VMEM_LIMIT),
        name="mixer_prompt",
    )(proj, gb, caw, canw, cqw, dnw)


SEQ_PER_STEP = 8
ROWS_PER_STEP = SEQ_PER_STEP * SAMPLE_LEN


def _mixer_sample_kernel(p_ref, gb_ref, bufa_ref, bufq_ref, caw_ref, canw_ref, cqw_ref, dnw_ref, s_ref,
                         y_ref, ua_ref, snew_ref,
                         w_scr, qd_scr, ub_scr, ke_scr, p_scr, u_scr, qs_scr, g_scr):
    j = pl.program_id(1)
    n = CHUNK

    @pl.when(j == 0)
    def _():
        rown = lax.broadcasted_iota(jnp.int32, (n, LANES), 0)
        tok = rown % SAMPLE_LEN

        def make_shift(buf_ref):
            def shift(x, s, cols):
                return jnp.where(tok >= s, pltpu.roll(x, s, 0),
                                 pltpu.roll(buf_ref[:, cols], n - (SAMPLE_LEN - s), 0))
            return shift

        shift_a = make_shift(bufa_ref)
        for gi in range(D_CONV // LANES):
            cols = slice(gi * LANES, (gi + 1) * LANES)
            u, y = _conv_a_group(p_ref, gi, caw_ref, canw_ref, shift_a)
            ua_ref[:, cols] = u
            y_ref[:, cols] = y

        shift_q = make_shift(bufq_ref)
        qkv = [None] * (3 * N_HEADS)
        for gi in range(3 * D_DN // LANES):
            _, qkv[gi] = _conv_qkv_group(p_ref, gi, cqw_ref, shift_q)

        row, col, incl, strict, eye = _group_masks(n, SAMPLE_LEN)
        inverse = functools.partial(_unit_lower_inverse, eye=eye, base=SAMPLE_LEN, base_mask=None, merge_masks=[])

        gb = gb_ref[...]
        gcum = gb
        s = 1
        while s < SAMPLE_LEN:
            gcum = gcum + jnp.where(tok >= s, pltpu.roll(gcum, s, 0), 0.0)
            s *= 2
        glast = gcum
        for back in range(1, SAMPLE_LEN):
            glast = jnp.where(tok == SAMPLE_LEN - 1 - back, pltpu.roll(gcum, n - back, 0), glast)
        g_scr[...] = gcum
        gcum_t = gcum.T
        e_end = jnp.exp(glast - gcum)
        heads = list(range(N_HEADS))
        gcol = [gcum[:, h:h + 1] for h in heads]
        bcol = [gb[:, N_HEADS + h:N_HEADS + h + 1] for h in heads]
        grow = [gcum_t[h:h + 1, :] for h in heads]
        ub, w, qd, p, _, _ = _chunk_prepare(qkv[:N_HEADS], qkv[N_HEADS:2 * N_HEADS], qkv[2 * N_HEADS:],
                                            gcol, bcol, grow, incl, strict, inverse)
        for h in heads:
            hc = slice(h * DK, (h + 1) * DK)
            ub_scr[:, hc] = ub[h]
            w_scr[:, hc] = w[h]
            qd_scr[:, hc] = qd[h]
            ke_scr[:, hc] = qkv[N_HEADS + h] * e_end[:, h:h + 1]
            p_scr[h] = p[h]
        u_scr[...] = jnp.zeros_like(u_scr)

    r0 = pl.multiple_of(j * ROWS_PER_STEP, ROWS_PER_STEP)
    row16 = lax.broadcasted_iota(jnp.int32, (2 * SUBLANES, LANES), 0)
    first_of_pair = (row16 % SUBLANES) < SAMPLE_LEN
    coln = lax.broadcasted_iota(jnp.int32, (DK, n), 1)
    heads = list(range(N_HEADS))
    hcs = [slice(h * DK, (h + 1) * DK) for h in heads]
    for h in heads:
        for t in range(ROWS_PER_STEP // SUBLANES):
            rows = pl.ds(pl.multiple_of(r0 + t * SUBLANES, SUBLANES), SUBLANES)
            lhs = jnp.concatenate([w_scr[rows, hcs[h]], qd_scr[rows, hcs[h]]], axis=0).astype(BF16)
            da = _dot_nt(lhs, s_ref[2 * t, h].astype(BF16))
            db = _dot_nt(lhs, s_ref[2 * t + 1, h].astype(BF16))
            ws = jnp.where(first_of_pair, da, db)
            u_scr[rows, hcs[h]] = ub_scr[rows, hcs[h]] - ws[:SUBLANES]
            qs_scr[rows, hcs[h]] = ws[SUBLANES:]
    u_h = [u_scr[:, hc] for hc in hcs]
    u_t = _each(lambda t: t.T, u_h)
    ke16 = [ke_scr[:, hc].astype(BF16) for hc in hcs]
    for sq in range(SEQ_PER_STEP):
        seq_cols = coln // SAMPLE_LEN == j * SEQ_PER_STEP + sq
        g_end = jnp.exp(g_scr[pl.ds(r0 + sq * SAMPLE_LEN + SAMPLE_LEN - 1, 1), :])
        upd = _each(lambda t, ke: _dot(jnp.where(seq_cols, t, 0.0).astype(BF16), ke), u_t, ke16)
        for h in heads:
            snew_ref[sq, h] = g_end[:, h:h + 1] * s_ref[sq, h] + upd[h]
    rows = pl.ds(r0, ROWS_PER_STEP)
    o = [qs_scr[rows, hcs[h]] + _dot(p_scr[h, rows, :].astype(BF16), u_h[h].astype(BF16)) for h in heads]
    for h in heads:
        y_ref[rows, D_CONV + h * DK:D_CONV + (h + 1) * DK] = _gated_out(
            o[h], p_ref[rows, Z_OFF + h * DK:Z_OFF + (h + 1) * DK], dnw_ref[...])


def _mixer_sample(proj, gb, bufa, bufq, caw, canw, cqw, dnw, state):
    m = proj.shape[0]
    n = CHUNK
    steps = n // ROWS_PER_STEP
    full = lambda shape: pl.BlockSpec(shape, lambda i, j: (0,) * len(shape))
    rowblk = lambda width: pl.BlockSpec((n, width), lambda i, j: (i, 0))
    sblk = pl.BlockSpec((SEQ_PER_STEP, N_HEADS, DK, DK), lambda i, j: (i * steps + j, 0, 0, 0))
    return pl.pallas_call(
        _mixer_sample_kernel,
        out_shape=(
            jax.ShapeDtypeStruct((m, D_MODEL), BF16),
            jax.ShapeDtypeStruct((m, D_CONV), F32),
            jax.ShapeDtypeStruct(state.shape, F32),
        ),
        grid=(m // n, steps),
        in_specs=[
            rowblk(D_MAIN), rowblk(LANES), rowblk(D_CONV), rowblk(3 * D_DN),
            full((SUBLANES, D_CONV)), full((1, D_CONV)), full((SUBLANES, 3 * D_DN)), full((1, DK)),
            sblk,
        ],
        out_specs=(rowblk(D_MODEL), rowblk(D_CONV), sblk),
        scratch_shapes=[
            pltpu.VMEM((n, D_DN), F32),
            pltpu.VMEM((n, D_DN), F32),
            pltpu.VMEM((n, D_DN), F32),
            pltpu.VMEM((n, D_DN), F32),
            pltpu.VMEM((N_HEADS, n, n), F32),
            pltpu.VMEM((n, D_DN), F32),
            pltpu.VMEM((n, D_DN), F32),
            pltpu.VMEM((n, LANES), F32),
        ],
        compiler_params=pltpu.CompilerParams(
            dimension_semantics=("arbitrary", "arbitrary"), vmem_limit_bytes=VMEM_LIMIT),
        name="mixer_sample",
    )(proj, gb, bufa, bufq, caw, canw, cqw, dnw, state)


def _pad_rows(w):
    return jnp.pad(w, ((0, 0), (0, SUBLANES - w.shape[1]), (0, 0)))


def kernel(x_prompt, x_sample, state_conv_a, state_conv_qkv, state_delta, norm_mix_w, w_in,
           conv_a_w, conv_a_norm_w, conv_qkv_w, a_log, dt_bias, dn_norm_w, w_out,
           norm_ffn_w, w_up, w_down, final_norm_w):
    depth = w_in.shape[0]
    n_seq, seq_len, _ = x_prompt.shape
    n_dec, dec_len, _ = x_sample.shape
    assert dec_len == SAMPLE_LEN

    w_main = w_in[:, :, :D_MAIN].astype(BF16)
    w_ab = jnp.pad(w_in[:, :, D_MAIN:], ((0, 0), (0, 0), (0, LANES - 2 * N_HEADS))).astype(BF16)
    w_out16, w_up16, w_down16 = w_out.astype(BF16), w_up.astype(BF16), w_down.astype(BF16)
    alog = jnp.pad(a_log, ((0, 0), (0, LANES - N_HEADS)))[:, None, :]
    dtb = jnp.pad(dt_bias, ((0, 0), (0, LANES - N_HEADS)))[:, None, :]
    caw, cqw = _pad_rows(conv_a_w), _pad_rows(conv_qkv_w)
    bufa = jnp.pad(state_conv_a, ((0, 0), (0, 0), (SAMPLE_LEN - state_conv_a.shape[2], 0), (0, 0))
                   ).reshape(depth, n_dec * SAMPLE_LEN, D_CONV)
    bufq = jnp.pad(state_conv_qkv, ((0, 0), (0, 0), (SAMPLE_LEN - state_conv_qkv.shape[2], 0), (0, 0))
                   ).reshape(depth, n_dec * SAMPLE_LEN, 3 * D_DN)

    xp = x_prompt.reshape(n_seq * seq_len, D_MODEL)
    xs = x_sample.reshape(n_dec * dec_len, D_MODEL)
    tm_p, tm_s = 1024, 512
    fw = final_norm_w[None, :]
    conv_a_p, conv_q_p, delta_p, conv_a_s, conv_q_s, delta_s = [], [], [], [], [], []
    for l in range(depth):
        nmw, nfw = norm_mix_w[l][None, :], norm_ffn_w[l][None, :]
        canw, dnw = conv_a_norm_w[l][None, :], dn_norm_w[l][None, :]
        final = l == depth - 1
        proj, gb = _proj(xp, nmw, w_main[l], w_ab[l], alog[l], dtb[l], tm=tm_p)
        y, last_a, last_q, s_fin = _mixer_prompt(proj, gb, caw[l], canw, cqw[l], dnw, n_seq=n_seq, seq_len=seq_len)
        xp = _outproj(y, w_out16[l], xp, tm=tm_p)
        xp = _ffn(xp, nfw, w_up16[l], w_down16[l], fw, tm=tm_p, final=final)
        conv_a_p.append(last_a[:, SUBLANES - 2:])
        conv_q_p.append(last_q[:, SUBLANES - 3:])
        delta_p.append(s_fin)
        proj, gb = _proj(xs, nmw, w_main[l], w_ab[l], alog[l], dtb[l], tm=tm_s)
        y, ua, s_new = _mixer_sample(proj, gb, bufa[l], bufq[l], caw[l], canw, cqw[l], dnw, state_delta[l])
        xs = _outproj(y, w_out16[l], xs, tm=tm_s)
        xs = _ffn(xs, nfw, w_up16[l], w_down16[l], fw, tm=tm_s, final=final)
        conv_a_s.append(ua.reshape(n_dec, SAMPLE_LEN, D_CONV)[:, SAMPLE_LEN - 2:])
        conv_q_s.append(proj[:, QKV_OFF:QKV_OFF + 3 * D_DN].reshape(n_dec, SAMPLE_LEN, 3 * D_DN)[:, SAMPLE_LEN - 3:])
        delta_s.append(s_new)

    return (xp.reshape(n_seq, seq_len, D_MODEL), xs.reshape(n_dec, dec_len, D_MODEL),
            jnp.stack(conv_a_p), jnp.stack(conv_q_p), jnp.stack(delta_p),
            jnp.stack(conv_a_s), jnp.stack(conv_q_s), jnp.stack(delta_s))
```

```python
import functools

import jax
import jax.numpy as jnp
from jax import lax
from jax.experimental import pallas as pl
from jax.experimental.pallas import tpu as pltpu

F32 = jnp.float32
BF16 = jnp.bfloat16

D_MODEL = 2048
D_CONV = 1024
D_DN = 1024
N_HEADS = 8
DK = 128
D_FF = 8192
D_MAIN = 3 * D_CONV + 4 * D_DN
QKV_OFF = 3 * D_CONV
Z_OFF = QKV_OFF + 3 * D_DN
EPS = 1e-6
LANES = 128
SUBLANES = 8
CHUNK = 128
SAMPLE_LEN = 4
VMEM_LIMIT = 56 * 1024 * 1024


def _dot(a, b):
    return jnp.dot(a, b, preferred_element_type=F32)


def _dot_nt(a, b):
    return lax.dot_general(a, b, (((1,), (1,)), ((), ())), preferred_element_type=F32)


def _softplus(x):
    return jnp.maximum(x, 0.0) + jnp.log1p(jnp.exp(-jnp.abs(x)))


def _silu(x):
    return x * jax.nn.sigmoid(x)


def _rms(x):
    return x * lax.rsqrt(jnp.mean(x * x, axis=-1, keepdims=True) + EPS)


def _weight_spec(w, layer, block, index):
    if w.ndim == 2:
        return pl.BlockSpec(block, index)
    return pl.BlockSpec((None,) + block, lambda *g: (layer,) + index(*g))


def _load_weight(w_ref, copy_ref):
    w = w_ref[...]
    if copy_ref is not None:
        w = w.astype(BF16)
        copy_ref[...] = w
    return w


def _proj_kernel(x_ref, nw_ref, w_ref, wab_ref, alog_ref, dtb_ref, proj_ref, gb_ref, *rest, tm, emit):
    w16_ref, h_scr = rest if emit else (None,) + rest

    @pl.when(pl.program_id(1) == 0)
    def _():
        for r in range(0, tm, 256):
            rows = slice(r, r + 256)
            h = (_rms(x_ref[rows, :]) * nw_ref[...]).astype(BF16)
            h_scr[rows, :] = h
            ab = _dot(h, wab_ref[...])
            lane = lax.broadcasted_iota(jnp.int32, ab.shape, 1)
            g = -jnp.exp(alog_ref[...]) * _softplus(ab + dtb_ref[...])
            gb_ref[rows, :] = jnp.where(lane < N_HEADS, g, jax.nn.sigmoid(ab))

    proj_ref[...] = _dot(h_scr[...], _load_weight(w_ref, w16_ref))


def _emits_bf16(w, m, tm):
    emit = w.dtype == F32
    assert not emit or m == tm
    return emit


def _proj(x, nw, w, wab, alog, dtb, *, layer, tm, tn=512):
    m = x.shape[0]
    emit = _emits_bf16(w, m, tm)
    out_shape = [jax.ShapeDtypeStruct((m, D_MAIN), F32), jax.ShapeDtypeStruct((m, LANES), F32)]
    out_specs = [pl.BlockSpec((tm, tn), lambda i, j: (i, j)), pl.BlockSpec((tm, LANES), lambda i, j: (i, 0))]
    if emit:
        out_shape.append(jax.ShapeDtypeStruct((D_MODEL, D_MAIN), BF16))
        out_specs.append(pl.BlockSpec((D_MODEL, tn), lambda i, j: (0, j)))
    outs = pl.pallas_call(
        functools.partial(_proj_kernel, tm=tm, emit=emit),
        out_shape=tuple(out_shape),
        grid=(m // tm, D_MAIN // tn),
        in_specs=[
            pl.BlockSpec((tm, D_MODEL), lambda i, j: (i, 0)),
            pl.BlockSpec((1, D_MODEL), lambda i, j: (0, 0)),
            _weight_spec(w, layer, (D_MODEL, tn), lambda i, j: (0, j)),
            pl.BlockSpec((None, D_MODEL, LANES), lambda i, j: (layer, 0, 0)),
            pl.BlockSpec((1, LANES), lambda i, j: (0, 0)),
            pl.BlockSpec((1, LANES), lambda i, j: (0, 0)),
        ],
        out_specs=tuple(out_specs),
        scratch_shapes=[pltpu.VMEM((tm, D_MODEL), BF16)],
        compiler_params=pltpu.CompilerParams(
            dimension_semantics=("arbitrary", "arbitrary"), vmem_limit_bytes=VMEM_LIMIT),
        name="norm_proj",
    )(x, nw, w, wab, alog, dtb)
    return outs if emit else (*outs, w)


def _outproj_kernel(y_ref, w_ref, x_ref, o_ref, w16_ref=None):
    o_ref[...] = x_ref[...] + _dot(y_ref[...], _load_weight(w_ref, w16_ref))


def _outproj(y, w, x, *, layer, tm, tn=512):
    m = x.shape[0]
    emit = _emits_bf16(w, m, tm)
    out_shape = [jax.ShapeDtypeStruct((m, D_MODEL), F32)]
    out_specs = [pl.BlockSpec((tm, tn), lambda i, j: (i, j))]
    if emit:
        out_shape.append(jax.ShapeDtypeStruct((D_MODEL, D_MODEL), BF16))
        out_specs.append(pl.BlockSpec((D_MODEL, tn), lambda i, j: (0, j)))
    outs = pl.pallas_call(
        _outproj_kernel,
        out_shape=tuple(out_shape),
        grid=(m // tm, D_MODEL // tn),
        in_specs=[
            pl.BlockSpec((tm, D_MODEL), lambda i, j: (i, 0)),
            _weight_spec(w, layer, (D_MODEL, tn), lambda i, j: (0, j)),
            pl.BlockSpec((tm, tn), lambda i, j: (i, j)),
        ],
        out_specs=tuple(out_specs),
        compiler_params=pltpu.CompilerParams(
            dimension_semantics=("arbitrary", "arbitrary"), vmem_limit_bytes=VMEM_LIMIT),
        name="out_proj",
    )(y, w, x)
    return outs if emit else (*outs, w)


def _ffn_kernel(x_ref, nw_ref, wu_ref, wd_ref, fw_ref, o_ref, *rest, tm, final, emit):
    wu16_ref, wd16_ref, h_scr = rest if emit else (None, None) + rest
    f = pl.program_id(1)

    @pl.when(f == 0)
    def _():
        for r in range(0, tm, 256):
            rows = slice(r, r + 256)
            x = x_ref[rows, :]
            h_scr[rows, :] = (_rms(x) * nw_ref[...]).astype(BF16)
            o_ref[rows, :] = x

    up = jnp.maximum(_dot(h_scr[...], _load_weight(wu_ref, wu16_ref)), 0.0)
    o_ref[...] += _dot((up * up).astype(BF16), _load_weight(wd_ref, wd16_ref))

    if final:
        @pl.when(f == pl.num_programs(1) - 1)
        def _():
            for r in range(0, tm, 256):
                rows = slice(r, r + 256)
                o_ref[rows, :] = _rms(o_ref[rows, :]) * fw_ref[...]


def _ffn(x, nw, wu, wd, fw, *, layer, tm, final, tf=512):
    m = x.shape[0]
    emit = _emits_bf16(wu, m, tm)
    assert (wd.dtype == F32) == emit
    out_shape = [jax.ShapeDtypeStruct((m, D_MODEL), F32)]
    out_specs = [pl.BlockSpec((tm, D_MODEL), lambda i, f: (i, 0))]
    if emit:
        out_shape += [jax.ShapeDtypeStruct((D_MODEL, D_FF), BF16), jax.ShapeDtypeStruct((D_FF, D_MODEL), BF16)]
        out_specs += [pl.BlockSpec((D_MODEL, tf), lambda i, f: (0, f)), pl.BlockSpec((tf, D_MODEL), lambda i, f: (f, 0))]
    outs = pl.pallas_call(
        functools.partial(_ffn_kernel, tm=tm, final=final, emit=emit),
        out_shape=tuple(out_shape),
        grid=(m // tm, D_FF // tf),
        in_specs=[
            pl.BlockSpec((tm, D_MODEL), lambda i, f: (i, 0)),
            pl.BlockSpec((1, D_MODEL), lambda i, f: (0, 0)),
            _weight_spec(wu, layer, (D_MODEL, tf), lambda i, f: (0, f)),
            _weight_spec(wd, layer, (tf, D_MODEL), lambda i, f: (f, 0)),
            pl.BlockSpec((1, D_MODEL), lambda i, f: (0, 0)),
        ],
        out_specs=tuple(out_specs),
        scratch_shapes=[pltpu.VMEM((tm, D_MODEL), BF16)],
        compiler_params=pltpu.CompilerParams(
            dimension_semantics=("arbitrary", "arbitrary"), vmem_limit_bytes=VMEM_LIMIT),
        name="ffn",
    )(x, nw, wu, wd, fw)
    return outs if emit else (*outs, wu, wd)


def _group_masks(n, group):
    row = lax.broadcasted_iota(jnp.int32, (n, n), 0)
    col = lax.broadcasted_iota(jnp.int32, (n, n), 1)
    same = (row // group) == (col // group)
    incl = same & (row >= col)
    strict = same & (row > col)
    eye = jnp.where(row == col, 1.0, 0.0).astype(F32)
    return row, col, incl, strict, eye


def _merge_masks(row, col, base, group):
    masks = []
    b = base
    while b < group:
        masks.append(((row // (2 * b)) == (col // (2 * b))) & ((row // b) != (col // b)))
        b *= 2
    return masks


def _each(f, *lists):
    return [f(*args) for args in zip(*lists)]


def _unit_lower_inverse(a, eye, base, base_mask, merge_masks):
    n1 = a if base_mask is None else _each(lambda t: jnp.where(base_mask, t, 0.0), a)
    n1b = _each(lambda t: t.astype(BF16), n1)
    n2 = _each(_dot, n1b, n1b)
    n2b = _each(lambda t: t.astype(BF16), n2)
    n3 = _each(_dot, n1b, n2b)
    x = _each(lambda p1, p2, p3: eye - p1 + p2 - p3, n1, n2, n3)
    if base == 8:
        n4b = _each(lambda t: _dot(t, t).astype(BF16), n2b)
        x = _each(lambda t, p4: t + _dot(t.astype(BF16), p4), x, n4b)
    for m in merge_masks:
        xb = _each(lambda t: t.astype(BF16), x)
        xl = _each(lambda tb, t: _dot(tb, jnp.where(m, t, 0.0).astype(BF16)).astype(BF16), xb, a)
        x = _each(lambda t, l, tb: t - _dot(l, tb), x, xl, xb)
    return x


def _chunk_prepare(q, k, v, gcol, bcol, grow, incl, strict, inverse):
    decay = _each(lambda gc, gr: jnp.exp(jnp.where(incl, gc - gr, -jnp.inf)), gcol, grow)
    kt = _each(lambda t: t.T, k)
    ktb = _each(lambda t: t.astype(BF16), kt)
    kk = _each(lambda t, tb: _dot(t.astype(BF16), tb), k, ktb)
    qk = _each(lambda t, tb: _dot(t.astype(BF16), tb), q, ktb)
    a = _each(lambda b, d, m: jnp.where(strict, b * d * m, 0.0), bcol, decay, kk)
    t = inverse(a)
    eg = _each(jnp.exp, gcol)
    rhs = _each(lambda b, e, vv, kx: jnp.concatenate([b * vv, (b * e) * kx], axis=1).astype(BF16), bcol, eg, v, k)
    sol = _each(lambda tt, r: _dot(tt.astype(BF16), r), t, rhs)
    return (_each(lambda s: s[:, :DK], sol), _each(lambda s: s[:, DK:], sol),
            _each(lambda qq, e: qq * e, q, eg), _each(lambda m, d: m * d, qk, decay), decay, kt)


def _gated_out(o, z, dnw):
    return (_rms(o) * dnw * _silu(z)).astype(BF16)


def _conv_a_group(p_ref, gi, caw_ref, canw_ref, shift):
    cols = slice(gi * LANES, (gi + 1) * LANES)
    c_cols = slice(D_CONV + gi * LANES, D_CONV + (gi + 1) * LANES)
    h_cols = slice(2 * D_CONV + gi * LANES, 2 * D_CONV + (gi + 1) * LANES)
    b_a = p_ref[:, cols]
    u = p_ref[:, c_cols] * p_ref[:, h_cols]
    conv = (caw_ref[2:3, cols] * u + caw_ref[1:2, cols] * shift(u, 1, cols)
            + caw_ref[0:1, cols] * shift(u, 2, cols))
    y = b_a * conv
    return u, (_rms(y) * canw_ref[:, cols]).astype(BF16)


def _conv_qkv_group(p_ref, gi, cqw_ref, shift):
    cols = slice(gi * LANES, (gi + 1) * LANES)
    x_cols = slice(QKV_OFF + gi * LANES, QKV_OFF + (gi + 1) * LANES)
    x = p_ref[:, x_cols]
    conv = (cqw_ref[3:4, cols] * x + cqw_ref[2:3, cols] * shift(x, 1, cols)
            + cqw_ref[1:2, cols] * shift(x, 2, cols) + cqw_ref[0:1, cols] * shift(x, 3, cols))
    c = _silu(conv)
    if gi < 2 * N_HEADS:
        c = c * lax.rsqrt(jnp.sum(c * c, axis=-1, keepdims=True) + EPS)
        if gi < N_HEADS:
            c = c * (DK ** -0.5)
    return x, c


def _mixer_prompt_kernel(p_ref, gb_ref, caw_ref, canw_ref, cqw_ref, dnw_ref,
                         y_ref, lasta_ref, lastq_ref, sfin_ref,
                         qkv_scr, ua_carry, xq_carry, st_scr, *, rb):
    nb = pl.program_id(1)

    @pl.when(nb == 0)
    def _():
        ua_carry[...] = jnp.zeros_like(ua_carry)
        xq_carry[...] = jnp.zeros_like(xq_carry)
        st_scr[...] = jnp.zeros_like(st_scr)

    row8 = lax.broadcasted_iota(jnp.int32, (SUBLANES, LANES), 0)

    def make_shift(carry_ref):
        def shift(x, s, cols):
            rolled = pltpu.roll(x, s, 0)
            prev = pltpu.roll(carry_ref[:, cols], s, 0)
            first = jnp.where(row8 < s, prev, rolled[:SUBLANES])
            return jnp.concatenate([first, rolled[SUBLANES:]], axis=0)
        return shift

    shift_a = make_shift(ua_carry)
    for gi in range(D_CONV // LANES):
        cols = slice(gi * LANES, (gi + 1) * LANES)
        u, y = _conv_a_group(p_ref, gi, caw_ref, canw_ref, shift_a)
        ua_carry[:, cols] = u[rb - SUBLANES:]
        y_ref[:, cols] = y

    shift_q = make_shift(xq_carry)
    for gi in range(3 * D_DN // LANES):
        cols = slice(gi * LANES, (gi + 1) * LANES)
        x, c = _conv_qkv_group(p_ref, gi, cqw_ref, shift_q)
        xq_carry[:, cols] = x[rb - SUBLANES:]
        qkv_scr[:, cols] = c

    row, col, incl, strict, eye = _group_masks(CHUNK, CHUNK)
    base_mask = (row // 8) == (col // 8)
    merges = _merge_masks(row, col, 8, CHUNK)
    inverse = functools.partial(_unit_lower_inverse, eye=eye, base=8, base_mask=base_mask, merge_masks=merges)
    rowc = lax.broadcasted_iota(jnp.int32, (CHUNK, LANES), 0)

    for c in range(rb // CHUNK):
        rows = slice(c * CHUNK, (c + 1) * CHUNK)
        gb = gb_ref[rows, :]
        gcum = gb
        s = 1
        while s < CHUNK:
            gcum = gcum + jnp.where(rowc >= s, pltpu.roll(gcum, s, 0), 0.0)
            s *= 2
        gcum_t = gcum.T
        heads = list(range(N_HEADS))
        q = [qkv_scr[rows, h * DK:(h + 1) * DK] for h in heads]
        k = [qkv_scr[rows, D_DN + h * DK:D_DN + (h + 1) * DK] for h in heads]
        v = [qkv_scr[rows, 2 * D_DN + h * DK:2 * D_DN + (h + 1) * DK] for h in heads]
        gcol = [gcum[:, h:h + 1] for h in heads]
        bcol = [gb[:, N_HEADS + h:N_HEADS + h + 1] for h in heads]
        grow = [gcum_t[h:h + 1, :] for h in heads]
        ub, w, qd, p, decay, kt = _chunk_prepare(q, k, v, gcol, bcol, grow, incl, strict, inverse)
        st = [st_scr[h] for h in heads]
        ws = _each(lambda a, b, s: _dot(jnp.concatenate([a, b], axis=0).astype(BF16), s.astype(BF16)), w, qd, st)
        u16 = _each(lambda a, b: (a - b[:CHUNK]).astype(BF16), ub, ws)
        o = _each(lambda a, b, c: a[CHUNK:] + _dot(b.astype(BF16), c), ws, p, u16)
        ket = _each(lambda a, d: (a * d[CHUNK - 1:CHUNK, :]).astype(BF16), kt, decay)
        st_new = _each(lambda g, s, a, b: jnp.exp(g[CHUNK - 1:CHUNK, :]) * s + _dot(a, b), gcol, st, ket, u16)
        for h in heads:
            st_scr[h] = st_new[h]
            y_ref[rows, D_CONV + h * DK:D_CONV + (h + 1) * DK] = _gated_out(
                o[h], p_ref[rows, Z_OFF + h * DK:Z_OFF + (h + 1) * DK], dnw_ref[...])

    @pl.when(nb == pl.num_programs(1) - 1)
    def _():
        lasta_ref[0] = ua_carry[...]
        lastq_ref[0] = xq_carry[...]
        for h in range(N_HEADS):
            sfin_ref[0, h] = st_scr[h].T


def _mixer_prompt(proj, gb, caw, canw, cqw, dnw, *, n_seq, seq_len, rb=256):
    nblk = seq_len // rb
    m = n_seq * seq_len
    full = lambda shape: pl.BlockSpec(shape, lambda b, n: (0,) * len(shape))
    return pl.pallas_call(
        functools.partial(_mixer_prompt_kernel, rb=rb),
        out_shape=(
            jax.ShapeDtypeStruct((m, D_MODEL), BF16),
            jax.ShapeDtypeStruct((n_seq, SUBLANES, D_CONV), F32),
            jax.ShapeDtypeStruct((n_seq, SUBLANES, 3 * D_DN), F32),
            jax.ShapeDtypeStruct((n_seq, N_HEADS, DK, DK), F32),
        ),
        grid=(n_seq, nblk),
        in_specs=[
            pl.BlockSpec((rb, D_MAIN), lambda b, n: (b * nblk + n, 0)),
            pl.BlockSpec((rb, LANES), lambda b, n: (b * nblk + n, 0)),
            full((SUBLANES, D_CONV)),
            full((1, D_CONV)),
            full((SUBLANES, 3 * D_DN)),
            full((1, DK)),
        ],
        out_specs=(
            pl.BlockSpec((rb, D_MODEL), lambda b, n: (b * nblk + n, 0)),
            pl.BlockSpec((1, SUBLANES, D_CONV), lambda b, n: (b, 0, 0)),
            pl.BlockSpec((1, SUBLANES, 3 * D_DN), lambda b, n: (b, 0, 0)),
            pl.BlockSpec((1, N_HEADS, DK, DK), lambda b, n: (b, 0, 0, 0)),
        ),
        scratch_shapes=[
            pltpu.VMEM((rb, 3 * D_DN), F32),
            pltpu.VMEM((SUBLANES, D_CONV), F32),
            pltpu.VMEM((SUBLANES, 3 * D_DN), F32),
            pltpu.VMEM((N_HEADS, DK, DK), F32),
        ],
        compiler_params=pltpu.CompilerParams(
            dimension_semantics=("arbitrary", "arbitrary"), vmem_limit_bytes=VMEM_LIMIT),
        name="mixer_prompt",
    )(proj, gb, caw, canw, cqw, dnw)


SEQ_PER_STEP = 8
ROWS_PER_STEP = SEQ_PER_STEP * SAMPLE_LEN


def _mixer_sample_kernel(p_ref, gb_ref, bufa_ref, bufq_ref, caw_ref, canw_ref, cqw_ref, dnw_ref, s_ref, prev_ref,
                         y_ref, ua_ref, snew_ref,
                         w_scr, qd_scr, ub_scr, ke_scr, p_scr, u_scr, qs_scr, g_scr):
    del prev_ref
    j = pl.program_id(1)
    n = CHUNK

    @pl.when(j == 0)
    def _():
        rown = lax.broadcasted_iota(jnp.int32, (n, LANES), 0)
        tok = rown % SAMPLE_LEN

        def make_shift(buf_ref):
            def shift(x, s, cols):
                return jnp.where(tok >= s, pltpu.roll(x, s, 0),
                                 pltpu.roll(buf_ref[:, cols], n - (SAMPLE_LEN - s), 0))
            return shift

        shift_a = make_shift(bufa_ref)
        for gi in range(D_CONV // LANES):
            cols = slice(gi * LANES, (gi + 1) * LANES)
            u, y = _conv_a_group(p_ref, gi, caw_ref, canw_ref, shift_a)
            ua_ref[:, cols] = u
            y_ref[:, cols] = y

        shift_q = make_shift(bufq_ref)
        qkv = [None] * (3 * N_HEADS)
        for gi in range(3 * D_DN // LANES):
            _, qkv[gi] = _conv_qkv_group(p_ref, gi, cqw_ref, shift_q)

        row, col, incl, strict, eye = _group_masks(n, SAMPLE_LEN)
        inverse = functools.partial(_unit_lower_inverse, eye=eye, base=SAMPLE_LEN, base_mask=None, merge_masks=[])

        gb = gb_ref[...]
        gcum = gb
        s = 1
        while s < SAMPLE_LEN:
            gcum = gcum + jnp.where(tok >= s, pltpu.roll(gcum, s, 0), 0.0)
            s *= 2
        glast = gcum
        for back in range(1, SAMPLE_LEN):
            glast = jnp.where(tok == SAMPLE_LEN - 1 - back, pltpu.roll(gcum, n - back, 0), glast)
        g_scr[...] = gcum
        gcum_t = gcum.T
        e_end = jnp.exp(glast - gcum)
        heads = list(range(N_HEADS))
        gcol = [gcum[:, h:h + 1] for h in heads]
        bcol = [gb[:, N_HEADS + h:N_HEADS + h + 1] for h in heads]
        grow = [gcum_t[h:h + 1, :] for h in heads]
        ub, w, qd, p, _, _ = _chunk_prepare(qkv[:N_HEADS], qkv[N_HEADS:2 * N_HEADS], qkv[2 * N_HEADS:],
                                            gcol, bcol, grow, incl, strict, inverse)
        for h in heads:
            hc = slice(h * DK, (h + 1) * DK)
            ub_scr[:, hc] = ub[h]
            w_scr[:, hc] = w[h]
            qd_scr[:, hc] = qd[h]
            ke_scr[:, hc] = qkv[N_HEADS + h] * e_end[:, h:h + 1]
            p_scr[h] = p[h]
        u_scr[...] = jnp.zeros_like(u_scr)

    r0 = pl.multiple_of(j * ROWS_PER_STEP, ROWS_PER_STEP)
    row16 = lax.broadcasted_iota(jnp.int32, (2 * SUBLANES, LANES), 0)
    first_of_pair = (row16 % SUBLANES) < SAMPLE_LEN
    coln = lax.broadcasted_iota(jnp.int32, (DK, n), 1)
    heads = list(range(N_HEADS))
    hcs = [slice(h * DK, (h + 1) * DK) for h in heads]
    for h in heads:
        for t in range(ROWS_PER_STEP // SUBLANES):
            rows = pl.ds(pl.multiple_of(r0 + t * SUBLANES, SUBLANES), SUBLANES)
            lhs = jnp.concatenate([w_scr[rows, hcs[h]], qd_scr[rows, hcs[h]]], axis=0).astype(BF16)
            da = _dot_nt(lhs, s_ref[2 * t, h].astype(BF16))
            db = _dot_nt(lhs, s_ref[2 * t + 1, h].astype(BF16))
            ws = jnp.where(first_of_pair, da, db)
            u_scr[rows, hcs[h]] = ub_scr[rows, hcs[h]] - ws[:SUBLANES]
            qs_scr[rows, hcs[h]] = ws[SUBLANES:]
    u_h = [u_scr[:, hc] for hc in hcs]
    u_t = _each(lambda t: t.T, u_h)
    ke16 = [ke_scr[:, hc].astype(BF16) for hc in hcs]
    for sq in range(SEQ_PER_STEP):
        seq_cols = coln // SAMPLE_LEN == j * SEQ_PER_STEP + sq
        g_end = jnp.exp(g_scr[pl.ds(r0 + sq * SAMPLE_LEN + SAMPLE_LEN - 1, 1), :])
        upd = _each(lambda t, ke: _dot(jnp.where(seq_cols, t, 0.0).astype(BF16), ke), u_t, ke16)
        for h in heads:
            snew_ref[sq, h] = g_end[:, h:h + 1] * s_ref[sq, h] + upd[h]
    rows = pl.ds(r0, ROWS_PER_STEP)
    o = [qs_scr[rows, hcs[h]] + _dot(p_scr[h, rows, :].astype(BF16), u_h[h].astype(BF16)) for h in heads]
    for h in heads:
        y_ref[rows, D_CONV + h * DK:D_CONV + (h + 1) * DK] = _gated_out(
            o[h], p_ref[rows, Z_OFF + h * DK:Z_OFF + (h + 1) * DK], dnw_ref[...])


def _mixer_sample(proj, gb, bufa, bufq, caw, canw, cqw, dnw, state, new_state, *, layer):
    m = proj.shape[0]
    n = CHUNK
    steps = n // ROWS_PER_STEP
    full = lambda shape: pl.BlockSpec(shape, lambda i, j: (0,) * len(shape))
    rowblk = lambda width: pl.BlockSpec((n, width), lambda i, j: (i, 0))
    sblk = pl.BlockSpec((None, SEQ_PER_STEP, N_HEADS, DK, DK), lambda i, j: (layer, i * steps + j, 0, 0, 0))
    prev_index = 9
    return pl.pallas_call(
        _mixer_sample_kernel,
        out_shape=(
            jax.ShapeDtypeStruct((m, D_MODEL), BF16),
            jax.ShapeDtypeStruct((m, D_CONV), F32),
            jax.ShapeDtypeStruct(state.shape, F32),
        ),
        grid=(m // n, steps),
        in_specs=[
            rowblk(D_MAIN), rowblk(LANES), rowblk(D_CONV), rowblk(3 * D_DN),
            full((SUBLANES, D_CONV)), full((1, D_CONV)), full((SUBLANES, 3 * D_DN)), full((1, DK)),
            sblk, pl.BlockSpec(memory_space=pl.ANY),
        ],
        input_output_aliases={prev_index: 2} if layer > 0 else {},
        out_specs=(rowblk(D_MODEL), rowblk(D_CONV), sblk),
        scratch_shapes=[
            pltpu.VMEM((n, D_DN), F32),
            pltpu.VMEM((n, D_DN), F32),
            pltpu.VMEM((n, D_DN), F32),
            pltpu.VMEM((n, D_DN), F32),
            pltpu.VMEM((N_HEADS, n, n), F32),
            pltpu.VMEM((n, D_DN), F32),
            pltpu.VMEM((n, D_DN), F32),
            pltpu.VMEM((n, LANES), F32),
        ],
        compiler_params=pltpu.CompilerParams(
            dimension_semantics=("arbitrary", "arbitrary"), vmem_limit_bytes=VMEM_LIMIT),
        name="mixer_sample",
    )(proj, gb, bufa, bufq, caw, canw, cqw, dnw, state, new_state)


def _pad_rows(w):
    return jnp.pad(w, ((0, 0), (0, SUBLANES - w.shape[1]), (0, 0)))


def kernel(x_prompt, x_sample, state_conv_a, state_conv_qkv, state_delta, norm_mix_w, w_in,
           conv_a_w, conv_a_norm_w, conv_qkv_w, a_log, dt_bias, dn_norm_w, w_out,
           norm_ffn_w, w_up, w_down, final_norm_w):
    depth = w_in.shape[0]
    n_seq, seq_len, _ = x_prompt.shape
    n_dec, dec_len, _ = x_sample.shape
    assert dec_len == SAMPLE_LEN

    w_ab = jnp.pad(w_in[:, :, D_MAIN:], ((0, 0), (0, 0), (0, LANES - 2 * N_HEADS))).astype(BF16)
    alog = jnp.pad(a_log, ((0, 0), (0, LANES - N_HEADS)))[:, None, :]
    dtb = jnp.pad(dt_bias, ((0, 0), (0, LANES - N_HEADS)))[:, None, :]
    caw, cqw = _pad_rows(conv_a_w), _pad_rows(conv_qkv_w)
    bufa = jnp.pad(state_conv_a, ((0, 0), (0, 0), (SAMPLE_LEN - state_conv_a.shape[2], 0), (0, 0))
                   ).reshape(depth, n_dec * SAMPLE_LEN, D_CONV)
    bufq = jnp.pad(state_conv_qkv, ((0, 0), (0, 0), (SAMPLE_LEN - state_conv_qkv.shape[2], 0), (0, 0))
                   ).reshape(depth, n_dec * SAMPLE_LEN, 3 * D_DN)

    xp = x_prompt.reshape(n_seq * seq_len, D_MODEL)
    xs = x_sample.reshape(n_dec * dec_len, D_MODEL)
    tm_p, tm_s = 1024, 512
    fw = final_norm_w[None, :]
    conv_a_p, conv_q_p, delta_p, conv_a_s, conv_q_s = [], [], [], [], []
    delta_s = state_delta
    for l in range(depth):
        nmw, nfw = norm_mix_w[l][None, :], norm_ffn_w[l][None, :]
        canw, dnw = conv_a_norm_w[l][None, :], dn_norm_w[l][None, :]
        final = l == depth - 1
        proj, gb, w_main16 = _proj(xs, nmw, w_in, w_ab, alog[l], dtb[l], layer=l, tm=tm_s)
        y, ua, delta_s = _mixer_sample(proj, gb, bufa[l], bufq[l], caw[l], canw, cqw[l], dnw, state_delta, delta_s,
                                       layer=l)
        xs, w_out16 = _outproj(y, w_out, xs, layer=l, tm=tm_s)
        xs, w_up16, w_down16 = _ffn(xs, nfw, w_up, w_down, fw, layer=l, tm=tm_s, final=final)
        conv_a_s.append(ua.reshape(n_dec, SAMPLE_LEN, D_CONV)[:, SAMPLE_LEN - 2:])
        conv_q_s.append(proj[:, QKV_OFF:QKV_OFF + 3 * D_DN].reshape(n_dec, SAMPLE_LEN, 3 * D_DN)[:, SAMPLE_LEN - 3:])
        proj, gb, _ = _proj(xp, nmw, w_main16, w_ab, alog[l], dtb[l], layer=l, tm=tm_p)
        y, last_a, last_q, s_fin = _mixer_prompt(proj, gb, caw[l], canw, cqw[l], dnw, n_seq=n_seq, seq_len=seq_len)
        xp, _ = _outproj(y, w_out16, xp, layer=l, tm=tm_p)
        xp, _, _ = _ffn(xp, nfw, w_up16, w_down16, fw, layer=l, tm=tm_p, final=final)
        conv_a_p.append(last_a[:, SUBLANES - 2:])
        conv_q_p.append(last_q[:, SUBLANES - 3:])
        delta_p.append(s_fin)

    return (xp.reshape(n_seq, seq_len, D_MODEL), xs.reshape(n_dec, dec_len, D_MODEL),
            jnp.stack(conv_a_p), jnp.stack(conv_q_p), jnp.stack(delta_p),
            jnp.stack(conv_a_s), jnp.stack(conv_q_s), delta_s)
```

```python
import functools

import jax
import jax.numpy as jnp
from jax import lax
from jax.experimental import pallas as pl
from jax.experimental.pallas import tpu as pltpu

F32 = jnp.float32
BF16 = jnp.bfloat16

D_MODEL = 2048
D_CONV = 1024
D_DN = 1024
N_HEADS = 8
DK = 128
D_FF = 8192
D_MAIN = 3 * D_CONV + 4 * D_DN
QKV_OFF = 3 * D_CONV
Z_OFF = QKV_OFF + 3 * D_DN
EPS = 1e-6
LANES = 128
SUBLANES = 8
CHUNK = 128
SAMPLE_LEN = 4
VMEM_LIMIT = 56 * 1024 * 1024


def _dot(a, b):
    return jnp.dot(a, b, preferred_element_type=F32)


def _dot_nt(a, b):
    return lax.dot_general(a, b, (((1,), (1,)), ((), ())), preferred_element_type=F32)


def _dot_tn(a, b):
    return lax.dot_general(a, b, (((0,), (0,)), ((), ())), preferred_element_type=F32)


def _softplus(x):
    return jnp.maximum(x, 0.0) + jnp.log1p(jnp.exp(-jnp.abs(x)))


def _silu(x):
    return x * jax.nn.sigmoid(x)


def _rms(x):
    return x * lax.rsqrt(jnp.mean(x * x, axis=-1, keepdims=True) + EPS)


def _weight_spec(w, layer, block, index):
    if w.ndim == 2:
        return pl.BlockSpec(block, index)
    return pl.BlockSpec((None,) + block, lambda *g: (layer,) + index(*g))


def _load_weight(w_ref, copy_ref):
    w = w_ref[...]
    if copy_ref is not None:
        w = w.astype(BF16)
        copy_ref[...] = w
    return w


def _proj_kernel(x_ref, nw_ref, w_ref, wab_ref, alog_ref, dtb_ref, proj_ref, gb_ref, *rest, tm, emit):
    w16_ref, h_scr, wab_scr = rest if emit else (None,) + rest

    @pl.when(pl.program_id(1) == 0)
    def _():
        wab_scr[...] = jnp.zeros_like(wab_scr)
        wab_scr[:2 * N_HEADS, :] = wab_ref[...].astype(BF16)
        for r in range(0, tm, 256):
            rows = slice(r, r + 256)
            h = (_rms(x_ref[rows, :]) * nw_ref[...]).astype(BF16)
            h_scr[rows, :] = h
            ab = _dot_nt(h, wab_scr[...])
            lane = lax.broadcasted_iota(jnp.int32, ab.shape, 1)
            g = -jnp.exp(alog_ref[...]) * _softplus(ab + dtb_ref[...])
            gb_ref[rows, :] = jnp.where(lane < N_HEADS, g, jax.nn.sigmoid(ab))

    proj_ref[...] = _dot_nt(h_scr[...], _load_weight(w_ref, w16_ref))


def _emits_bf16(w, m, tm):
    emit = w.dtype == F32
    assert not emit or m == tm
    return emit


def _proj(x, nw, w, w_in_t, alog, dtb, *, layer, tm, tn=512):
    m = x.shape[0]
    emit = _emits_bf16(w, m, tm)
    n_gate = 2 * N_HEADS
    out_shape = [jax.ShapeDtypeStruct((m, D_MAIN), F32), jax.ShapeDtypeStruct((m, LANES), F32)]
    out_specs = [pl.BlockSpec((tm, tn), lambda i, j: (i, j)), pl.BlockSpec((tm, LANES), lambda i, j: (i, 0))]
    if emit:
        out_shape.append(jax.ShapeDtypeStruct((D_MAIN, D_MODEL), BF16))
        out_specs.append(pl.BlockSpec((tn, D_MODEL), lambda i, j: (j, 0)))
    outs = pl.pallas_call(
        functools.partial(_proj_kernel, tm=tm, emit=emit),
        out_shape=tuple(out_shape),
        grid=(m // tm, D_MAIN // tn),
        in_specs=[
            pl.BlockSpec((tm, D_MODEL), lambda i, j: (i, 0)),
            pl.BlockSpec((1, D_MODEL), lambda i, j: (0, 0)),
            _weight_spec(w, layer, (tn, D_MODEL), lambda i, j: (j, 0)),
            pl.BlockSpec((None, n_gate, D_MODEL), lambda i, j: (layer, D_MAIN // n_gate, 0)),
            pl.BlockSpec((1, LANES), lambda i, j: (0, 0)),
            pl.BlockSpec((1, LANES), lambda i, j: (0, 0)),
        ],
        out_specs=tuple(out_specs),
        scratch_shapes=[pltpu.VMEM((tm, D_MODEL), BF16), pltpu.VMEM((LANES, D_MODEL), BF16)],
        compiler_params=pltpu.CompilerParams(
            dimension_semantics=("arbitrary", "arbitrary"), vmem_limit_bytes=VMEM_LIMIT),
        name="norm_proj",
    )(x, nw, w, w_in_t, alog, dtb)
    return outs if emit else (*outs, w)


def _outproj_kernel(y_ref, w_ref, x_ref, o_ref, w16_ref=None):
    o_ref[...] = x_ref[...] + _dot(y_ref[...], _load_weight(w_ref, w16_ref))


def _outproj(y, w, x, *, layer, tm, tn=512):
    m = x.shape[0]
    emit = _emits_bf16(w, m, tm)
    out_shape = [jax.ShapeDtypeStruct((m, D_MODEL), F32)]
    out_specs = [pl.BlockSpec((tm, tn), lambda i, j: (i, j))]
    if emit:
        out_shape.append(jax.ShapeDtypeStruct((D_MODEL, D_MODEL), BF16))
        out_specs.append(pl.BlockSpec((D_MODEL, tn), lambda i, j: (0, j)))
    outs = pl.pallas_call(
        _outproj_kernel,
        out_shape=tuple(out_shape),
        grid=(m // tm, D_MODEL // tn),
        in_specs=[
            pl.BlockSpec((tm, D_MODEL), lambda i, j: (i, 0)),
            _weight_spec(w, layer, (D_MODEL, tn), lambda i, j: (0, j)),
            pl.BlockSpec((tm, tn), lambda i, j: (i, j)),
        ],
        out_specs=tuple(out_specs),
        compiler_params=pltpu.CompilerParams(
            dimension_semantics=("arbitrary", "arbitrary"), vmem_limit_bytes=VMEM_LIMIT),
        name="out_proj",
    )(y, w, x)
    return outs if emit else (*outs, w)


def _ffn_kernel(x_ref, nw_ref, wu_ref, wd_ref, fw_ref, o_ref, *rest, tm, final, emit):
    wu16_ref, wd16_ref, h_scr = rest if emit else (None, None) + rest
    f = pl.program_id(1)

    @pl.when(f == 0)
    def _():
        for r in range(0, tm, 256):
            rows = slice(r, r + 256)
            x = x_ref[rows, :]
            h_scr[rows, :] = (_rms(x) * nw_ref[...]).astype(BF16)
            o_ref[rows, :] = x

    up = jnp.maximum(_dot(h_scr[...], _load_weight(wu_ref, wu16_ref)), 0.0)
    o_ref[...] += _dot((up * up).astype(BF16), _load_weight(wd_ref, wd16_ref))

    if final:
        @pl.when(f == pl.num_programs(1) - 1)
        def _():
            for r in range(0, tm, 256):
                rows = slice(r, r + 256)
                o_ref[rows, :] = _rms(o_ref[rows, :]) * fw_ref[...]


def _ffn(x, nw, wu, wd, fw, *, layer, tm, final, tf=512):
    m = x.shape[0]
    emit = _emits_bf16(wu, m, tm)
    assert (wd.dtype == F32) == emit
    out_shape = [jax.ShapeDtypeStruct((m, D_MODEL), F32)]
    out_specs = [pl.BlockSpec((tm, D_MODEL), lambda i, f: (i, 0))]
    if emit:
        out_shape += [jax.ShapeDtypeStruct((D_MODEL, D_FF), BF16), jax.ShapeDtypeStruct((D_FF, D_MODEL), BF16)]
        out_specs += [pl.BlockSpec((D_MODEL, tf), lambda i, f: (0, f)), pl.BlockSpec((tf, D_MODEL), lambda i, f: (f, 0))]
    outs = pl.pallas_call(
        functools.partial(_ffn_kernel, tm=tm, final=final, emit=emit),
        out_shape=tuple(out_shape),
        grid=(m // tm, D_FF // tf),
        in_specs=[
            pl.BlockSpec((tm, D_MODEL), lambda i, f: (i, 0)),
            pl.BlockSpec((1, D_MODEL), lambda i, f: (0, 0)),
            _weight_spec(wu, layer, (D_MODEL, tf), lambda i, f: (0, f)),
            _weight_spec(wd, layer, (tf, D_MODEL), lambda i, f: (f, 0)),
            pl.BlockSpec((1, D_MODEL), lambda i, f: (0, 0)),
        ],
        out_specs=tuple(out_specs),
        scratch_shapes=[pltpu.VMEM((tm, D_MODEL), BF16)],
        compiler_params=pltpu.CompilerParams(
            dimension_semantics=("arbitrary", "arbitrary"), vmem_limit_bytes=VMEM_LIMIT),
        name="ffn",
    )(x, nw, wu, wd, fw)
    return outs if emit else (*outs, wu, wd)


def _group_masks(n, group):
    row = lax.broadcasted_iota(jnp.int32, (n, n), 0)
    col = lax.broadcasted_iota(jnp.int32, (n, n), 1)
    same = (row // group) == (col // group)
    incl = same & (row >= col)
    strict = same & (row > col)
    eye = jnp.where(row == col, 1.0, 0.0).astype(F32)
    return row, col, incl, strict, eye


def _merge_masks(row, col, base, group):
    masks = []
    b = base
    while b < group:
        masks.append(((row // (2 * b)) == (col // (2 * b))) & ((row // b) != (col // b)))
        b *= 2
    return masks


def _each(f, *lists):
    return [f(*args) for args in zip(*lists)]


def _unit_lower_inverse(a, eye, base, base_mask, merge_masks):
    n1 = a if base_mask is None else _each(lambda t: jnp.where(base_mask, t, 0.0), a)
    n1b = _each(lambda t: t.astype(BF16), n1)
    n2 = _each(_dot, n1b, n1b)
    n2b = _each(lambda t: t.astype(BF16), n2)
    n3 = _each(_dot, n1b, n2b)
    x = _each(lambda p1, p2, p3: eye - p1 + p2 - p3, n1, n2, n3)
    if base == 8:
        n4b = _each(lambda t: _dot(t, t).astype(BF16), n2b)
        x = _each(lambda t, p4: t + _dot(t.astype(BF16), p4), x, n4b)
    for m in merge_masks:
        xb = _each(lambda t: t.astype(BF16), x)
        xl = _each(lambda tb, t: _dot(tb, jnp.where(m, t, 0.0).astype(BF16)).astype(BF16), xb, a)
        x = _each(lambda t, l, tb: t - _dot(l, tb), x, xl, xb)
    return x


def _chunk_prepare(q, k, v, gcol, bcol, grow, incl, strict, inverse):
    decay = _each(lambda gc, gr: jnp.exp(jnp.where(incl, gc - gr, -jnp.inf)), gcol, grow)
    kb = _each(lambda t: t.astype(BF16), k)
    kk = _each(_dot_nt, kb, kb)
    qk = _each(lambda t, tb: _dot_nt(t.astype(BF16), tb), q, kb)
    a = _each(lambda b, d, m: jnp.where(strict, b * d * m, 0.0), bcol, decay, kk)
    t = inverse(a)
    eg = _each(jnp.exp, gcol)
    rhs = _each(lambda b, e, vv, kx: jnp.concatenate([b * vv, (b * e) * kx], axis=1).astype(BF16), bcol, eg, v, k)
    sol = _each(lambda tt, r: _dot(tt.astype(BF16), r), t, rhs)
    return (_each(lambda s: s[:, :DK], sol), _each(lambda s: s[:, DK:], sol),
            _each(lambda qq, e: qq * e, q, eg), _each(lambda m, d: m * d, qk, decay))


def _gated_out(o, z, dnw):
    return (_rms(o) * dnw * _silu(z)).astype(BF16)


def _conv_a_group(p_ref, gi, caw_ref, canw_ref, shift):
    cols = slice(gi * LANES, (gi + 1) * LANES)
    c_cols = slice(D_CONV + gi * LANES, D_CONV + (gi + 1) * LANES)
    h_cols = slice(2 * D_CONV + gi * LANES, 2 * D_CONV + (gi + 1) * LANES)
    b_a = p_ref[:, cols]
    u = p_ref[:, c_cols] * p_ref[:, h_cols]
    conv = (caw_ref[2:3, cols] * u + caw_ref[1:2, cols] * shift(u, 1, cols)
            + caw_ref[0:1, cols] * shift(u, 2, cols))
    y = b_a * conv
    return u, (_rms(y) * canw_ref[:, cols]).astype(BF16)


def _conv_qkv_group(p_ref, gi, cqw_ref, shift):
    cols = slice(gi * LANES, (gi + 1) * LANES)
    x_cols = slice(QKV_OFF + gi * LANES, QKV_OFF + (gi + 1) * LANES)
    x = p_ref[:, x_cols]
    conv = (cqw_ref[3:4, cols] * x + cqw_ref[2:3, cols] * shift(x, 1, cols)
            + cqw_ref[1:2, cols] * shift(x, 2, cols) + cqw_ref[0:1, cols] * shift(x, 3, cols))
    c = _silu(conv)
    if gi < 2 * N_HEADS:
        c = c * lax.rsqrt(jnp.sum(c * c, axis=-1, keepdims=True) + EPS)
        if gi < N_HEADS:
            c = c * (DK ** -0.5)
    return x, c


def _mixer_prompt_kernel(p_ref, gb_ref, caw_ref, canw_ref, cqw_ref, dnw_ref,
                         y_ref, lasta_ref, lastq_ref, sfin_ref,
                         qkv_scr, ua_carry, xq_carry, st_scr, *, rb):
    nb = pl.program_id(1)

    @pl.when(nb == 0)
    def _():
        ua_carry[...] = jnp.zeros_like(ua_carry)
        xq_carry[...] = jnp.zeros_like(xq_carry)
        st_scr[...] = jnp.zeros_like(st_scr)

    row8 = lax.broadcasted_iota(jnp.int32, (SUBLANES, LANES), 0)

    def make_shift(carry_ref):
        def shift(x, s, cols):
            rolled = pltpu.roll(x, s, 0)
            prev = pltpu.roll(carry_ref[:, cols], s, 0)
            first = jnp.where(row8 < s, prev, rolled[:SUBLANES])
            return jnp.concatenate([first, rolled[SUBLANES:]], axis=0)
        return shift

    shift_a = make_shift(ua_carry)
    for gi in range(D_CONV // LANES):
        cols = slice(gi * LANES, (gi + 1) * LANES)
        u, y = _conv_a_group(p_ref, gi, caw_ref, canw_ref, shift_a)
        ua_carry[:, cols] = u[rb - SUBLANES:]
        y_ref[:, cols] = y

    shift_q = make_shift(xq_carry)
    for gi in range(3 * D_DN // LANES):
        cols = slice(gi * LANES, (gi + 1) * LANES)
        x, c = _conv_qkv_group(p_ref, gi, cqw_ref, shift_q)
        xq_carry[:, cols] = x[rb - SUBLANES:]
        qkv_scr[:, cols] = c

    row, col, incl, strict, eye = _group_masks(CHUNK, CHUNK)
    base_mask = (row // 8) == (col // 8)
    merges = _merge_masks(row, col, 8, CHUNK)
    inverse = functools.partial(_unit_lower_inverse, eye=eye, base=8, base_mask=base_mask, merge_masks=merges)
    rowc = lax.broadcasted_iota(jnp.int32, (CHUNK, LANES), 0)

    heads = list(range(N_HEADS))
    chunks = list(range(rb // CHUNK))
    q, k, v, gcol, bcol, grow = [], [], [], [], [], []
    for c in chunks:
        rows = slice(c * CHUNK, (c + 1) * CHUNK)
        gb = gb_ref[rows, :]
        gcum = gb
        s = 1
        while s < CHUNK:
            gcum = gcum + jnp.where(rowc >= s, pltpu.roll(gcum, s, 0), 0.0)
            s *= 2
        gcum_t = gcum.T
        q += [qkv_scr[rows, h * DK:(h + 1) * DK] for h in heads]
        k += [qkv_scr[rows, D_DN + h * DK:D_DN + (h + 1) * DK] for h in heads]
        v += [qkv_scr[rows, 2 * D_DN + h * DK:2 * D_DN + (h + 1) * DK] for h in heads]
        gcol += [gcum[:, h:h + 1] for h in heads]
        bcol += [gb[:, N_HEADS + h:N_HEADS + h + 1] for h in heads]
        grow += [gcum_t[h:h + 1, :] for h in heads]
    ub, w, qd, p = _chunk_prepare(q, k, v, gcol, bcol, grow, incl, strict, inverse)
    ke = _each(lambda a, g: (a * jnp.exp(g[CHUNK - 1:CHUNK, :] - g)).astype(BF16), k, gcol)
    lhs = _each(lambda a, b: jnp.concatenate([a, b], axis=0).astype(BF16), w, qd)
    p16 = _each(lambda a: a.astype(BF16), p)
    g_end = _each(lambda g: jnp.exp(g[CHUNK - 1:CHUNK, :]), gcol)

    st = [st_scr[h] for h in heads]
    for c in chunks:
        rows = slice(c * CHUNK, (c + 1) * CHUNK)
        sl = slice(c * N_HEADS, (c + 1) * N_HEADS)
        ws = _each(lambda a, s: _dot(a, s.astype(BF16)), lhs[sl], st)
        u16 = _each(lambda a, b: (a - b[:CHUNK]).astype(BF16), ub[sl], ws)
        o = _each(lambda a, b, c: a[CHUNK:] + _dot(b, c), ws, p16[sl], u16)
        st = _each(lambda g, s, a, b: g * s + _dot_tn(a, b), g_end[sl], st, ke[sl], u16)
        for h in heads:
            y_ref[rows, D_CONV + h * DK:D_CONV + (h + 1) * DK] = _gated_out(
                o[h], p_ref[rows, Z_OFF + h * DK:Z_OFF + (h + 1) * DK], dnw_ref[...])
    for h in heads:
        st_scr[h] = st[h]

    @pl.when(nb == pl.num_programs(1) - 1)
    def _():
        lasta_ref[0] = ua_carry[...]
        lastq_ref[0] = xq_carry[...]
        for h in range(N_HEADS):
            sfin_ref[0, h] = st_scr[h].T


def _mixer_prompt(proj, gb, caw, canw, cqw, dnw, *, n_seq, seq_len, rb=256):
    nblk = seq_len // rb
    m = n_seq * seq_len
    full = lambda shape: pl.BlockSpec(shape, lambda b, n: (0,) * len(shape))
    return pl.pallas_call(
        functools.partial(_mixer_prompt_kernel, rb=rb),
        out_shape=(
            jax.ShapeDtypeStruct((m, D_MODEL), BF16),
            jax.ShapeDtypeStruct((n_seq, SUBLANES, D_CONV), F32),
            jax.ShapeDtypeStruct((n_seq, SUBLANES, 3 * D_DN), F32),
            jax.ShapeDtypeStruct((n_seq, N_HEADS, DK, DK), F32),
        ),
        grid=(n_seq, nblk),
        in_specs=[
            pl.BlockSpec((rb, D_MAIN), lambda b, n: (b * nblk + n, 0)),
            pl.BlockSpec((rb, LANES), lambda b, n: (b * nblk + n, 0)),
            full((SUBLANES, D_CONV)),
            full((1, D_CONV)),
            full((SUBLANES, 3 * D_DN)),
            full((1, DK)),
        ],
        out_specs=(
            pl.BlockSpec((rb, D_MODEL), lambda b, n: (b * nblk + n, 0)),
            pl.BlockSpec((1, SUBLANES, D_CONV), lambda b, n: (b, 0, 0)),
            pl.BlockSpec((1, SUBLANES, 3 * D_DN), lambda b, n: (b, 0, 0)),
            pl.BlockSpec((1, N_HEADS, DK, DK), lambda b, n: (b, 0, 0, 0)),
        ),
        scratch_shapes=[
            pltpu.VMEM((rb, 3 * D_DN), F32),
            pltpu.VMEM((SUBLANES, D_CONV), F32),
            pltpu.VMEM((SUBLANES, 3 * D_DN), F32),
            pltpu.VMEM((N_HEADS, DK, DK), F32),
        ],
        compiler_params=pltpu.CompilerParams(
            dimension_semantics=("arbitrary", "arbitrary"), vmem_limit_bytes=VMEM_LIMIT),
        name="mixer_prompt",
    )(proj, gb, caw, canw, cqw, dnw)


SEQ_PER_STEP = 8
ROWS_PER_STEP = SEQ_PER_STEP * SAMPLE_LEN


def _mixer_sample_kernel(p_ref, gb_ref, bufa_ref, bufq_ref, caw_ref, canw_ref, cqw_ref, dnw_ref, s_ref, prev_ref,
                         y_ref, ua_ref, snew_ref,
                         w_scr, qd_scr, ub_scr, ke_scr, p_scr, u_scr, qs_scr, g_scr):
    del prev_ref
    j = pl.program_id(1)
    n = CHUNK

    @pl.when(j == 0)
    def _():
        rown = lax.broadcasted_iota(jnp.int32, (n, LANES), 0)
        tok = rown % SAMPLE_LEN

        def make_shift(buf_ref):
            def shift(x, s, cols):
                return jnp.where(tok >= s, pltpu.roll(x, s, 0),
                                 pltpu.roll(buf_ref[:, cols], n - (SAMPLE_LEN - s), 0))
            return shift

        shift_a = make_shift(bufa_ref)
        for gi in range(D_CONV // LANES):
            cols = slice(gi * LANES, (gi + 1) * LANES)
            u, y = _conv_a_group(p_ref, gi, caw_ref, canw_ref, shift_a)
            ua_ref[:, cols] = u
            y_ref[:, cols] = y

        shift_q = make_shift(bufq_ref)
        qkv = [None] * (3 * N_HEADS)
        for gi in range(3 * D_DN // LANES):
            _, qkv[gi] = _conv_qkv_group(p_ref, gi, cqw_ref, shift_q)

        row, col, incl, strict, eye = _group_masks(n, SAMPLE_LEN)
        inverse = functools.partial(_unit_lower_inverse, eye=eye, base=SAMPLE_LEN, base_mask=None, merge_masks=[])

        gb = gb_ref[...]
        gcum = gb
        s = 1
        while s < SAMPLE_LEN:
            gcum = gcum + jnp.where(tok >= s, pltpu.roll(gcum, s, 0), 0.0)
            s *= 2
        glast = gcum
        for back in range(1, SAMPLE_LEN):
            glast = jnp.where(tok == SAMPLE_LEN - 1 - back, pltpu.roll(gcum, n - back, 0), glast)
        g_scr[...] = gcum
        gcum_t = gcum.T
        e_end = jnp.exp(glast - gcum)
        heads = list(range(N_HEADS))
        gcol = [gcum[:, h:h + 1] for h in heads]
        bcol = [gb[:, N_HEADS + h:N_HEADS + h + 1] for h in heads]
        grow = [gcum_t[h:h + 1, :] for h in heads]
        ub, w, qd, p = _chunk_prepare(qkv[:N_HEADS], qkv[N_HEADS:2 * N_HEADS], qkv[2 * N_HEADS:],
                                            gcol, bcol, grow, incl, strict, inverse)
        for h in heads:
            hc = slice(h * DK, (h + 1) * DK)
            ub_scr[:, hc] = ub[h]
            w_scr[:, hc] = w[h]
            qd_scr[:, hc] = qd[h]
            ke_scr[:, hc] = qkv[N_HEADS + h] * e_end[:, h:h + 1]
            p_scr[h] = p[h]
        u_scr[...] = jnp.zeros_like(u_scr)

    r0 = pl.multiple_of(j * ROWS_PER_STEP, ROWS_PER_STEP)
    row16 = lax.broadcasted_iota(jnp.int32, (2 * SUBLANES, LANES), 0)
    first_of_pair = (row16 % SUBLANES) < SAMPLE_LEN
    coln = lax.broadcasted_iota(jnp.int32, (DK, n), 1)
    heads = list(range(N_HEADS))
    hcs = [slice(h * DK, (h + 1) * DK) for h in heads]
    for h in heads:
        for t in range(ROWS_PER_STEP // SUBLANES):
            rows = pl.ds(pl.multiple_of(r0 + t * SUBLANES, SUBLANES), SUBLANES)
            lhs = jnp.concatenate([w_scr[rows, hcs[h]], qd_scr[rows, hcs[h]]], axis=0).astype(BF16)
            da = _dot_nt(lhs, s_ref[2 * t, h].astype(BF16))
            db = _dot_nt(lhs, s_ref[2 * t + 1, h].astype(BF16))
            ws = jnp.where(first_of_pair, da, db)
            u_scr[rows, hcs[h]] = ub_scr[rows, hcs[h]] - ws[:SUBLANES]
            qs_scr[rows, hcs[h]] = ws[SUBLANES:]
    u_h = [u_scr[:, hc] for hc in hcs]
    u_t = _each(lambda t: t.T.astype(BF16), u_h)
    ke16 = [ke_scr[:, hc].astype(BF16) for hc in hcs]
    for sq in range(SEQ_PER_STEP):
        seq_cols = jnp.where(coln // SAMPLE_LEN == j * SEQ_PER_STEP + sq, 1.0, 0.0).astype(BF16)
        g_end = jnp.exp(g_scr[pl.ds(r0 + sq * SAMPLE_LEN + SAMPLE_LEN - 1, 1), :])
        upd = _each(lambda t, ke: _dot(t * seq_cols, ke), u_t, ke16)
        for h in heads:
            snew_ref[sq, h] = g_end[:, h:h + 1] * s_ref[sq, h] + upd[h]
    rows = pl.ds(r0, ROWS_PER_STEP)
    o = [qs_scr[rows, hcs[h]] + _dot(p_scr[h, rows, :].astype(BF16), u_h[h].astype(BF16)) for h in heads]
    for h in heads:
        y_ref[rows, D_CONV + h * DK:D_CONV + (h + 1) * DK] = _gated_out(
            o[h], p_ref[rows, Z_OFF + h * DK:Z_OFF + (h + 1) * DK], dnw_ref[...])


def _mixer_sample(proj, gb, bufa, bufq, caw, canw, cqw, dnw, state, new_state, *, layer):
    m = proj.shape[0]
    n = CHUNK
    steps = n // ROWS_PER_STEP
    full = lambda shape: pl.BlockSpec(shape, lambda i, j: (0,) * len(shape))
    rowblk = lambda width: pl.BlockSpec((n, width), lambda i, j: (i, 0))
    sblk = pl.BlockSpec((None, SEQ_PER_STEP, N_HEADS, DK, DK), lambda i, j: (layer, i * steps + j, 0, 0, 0))
    prev_index = 9
    return pl.pallas_call(
        _mixer_sample_kernel,
        out_shape=(
            jax.ShapeDtypeStruct((m, D_MODEL), BF16),
            jax.ShapeDtypeStruct((m, D_CONV), F32),
            jax.ShapeDtypeStruct(state.shape, F32),
        ),
        grid=(m // n, steps),
        in_specs=[
            rowblk(D_MAIN), rowblk(LANES), rowblk(D_CONV), rowblk(3 * D_DN),
            full((SUBLANES, D_CONV)), full((1, D_CONV)), full((SUBLANES, 3 * D_DN)), full((1, DK)),
            sblk, pl.BlockSpec(memory_space=pl.ANY),
        ],
        input_output_aliases={prev_index: 2} if layer > 0 else {},
        out_specs=(rowblk(D_MODEL), rowblk(D_CONV), sblk),
        scratch_shapes=[
            pltpu.VMEM((n, D_DN), F32),
            pltpu.VMEM((n, D_DN), F32),
            pltpu.VMEM((n, D_DN), F32),
            pltpu.VMEM((n, D_DN), F32),
            pltpu.VMEM((N_HEADS, n, n), F32),
            pltpu.VMEM((n, D_DN), F32),
            pltpu.VMEM((n, D_DN), F32),
            pltpu.VMEM((n, LANES), F32),
        ],
        compiler_params=pltpu.CompilerParams(
            dimension_semantics=("arbitrary", "arbitrary"), vmem_limit_bytes=VMEM_LIMIT),
        name="mixer_sample",
    )(proj, gb, bufa, bufq, caw, canw, cqw, dnw, state, new_state)


def _pad_rows(w):
    return jnp.pad(w, ((0, 0), (0, SUBLANES - w.shape[1]), (0, 0)))


def kernel(x_prompt, x_sample, state_conv_a, state_conv_qkv, state_delta, norm_mix_w, w_in,
           conv_a_w, conv_a_norm_w, conv_qkv_w, a_log, dt_bias, dn_norm_w, w_out,
           norm_ffn_w, w_up, w_down, final_norm_w):
    depth = w_in.shape[0]
    n_seq, seq_len, _ = x_prompt.shape
    n_dec, dec_len, _ = x_sample.shape
    assert dec_len == SAMPLE_LEN

    w_in_t = jnp.swapaxes(w_in, 1, 2)
    alog = jnp.pad(a_log, ((0, 0), (0, LANES - N_HEADS)))[:, None, :]
    dtb = jnp.pad(dt_bias, ((0, 0), (0, LANES - N_HEADS)))[:, None, :]
    caw, cqw = _pad_rows(conv_a_w), _pad_rows(conv_qkv_w)
    bufa = jnp.pad(state_conv_a, ((0, 0), (0, 0), (SAMPLE_LEN - state_conv_a.shape[2], 0), (0, 0))
                   ).reshape(depth, n_dec * SAMPLE_LEN, D_CONV)
    bufq = jnp.pad(state_conv_qkv, ((0, 0), (0, 0), (SAMPLE_LEN - state_conv_qkv.shape[2], 0), (0, 0))
                   ).reshape(depth, n_dec * SAMPLE_LEN, 3 * D_DN)

    xp = x_prompt.reshape(n_seq * seq_len, D_MODEL)
    xs = x_sample.reshape(n_dec * dec_len, D_MODEL)
    tm_p, tm_s = 1024, 512
    fw = final_norm_w[None, :]
    conv_a_p, conv_q_p, delta_p, conv_a_s, conv_q_s = [], [], [], [], []
    delta_s = state_delta
    for l in range(depth):
        nmw, nfw = norm_mix_w[l][None, :], norm_ffn_w[l][None, :]
        canw, dnw = conv_a_norm_w[l][None, :], dn_norm_w[l][None, :]
        final = l == depth - 1
        proj, gb, w_main16 = _proj(xs, nmw, w_in_t, w_in_t, alog[l], dtb[l], layer=l, tm=tm_s)
        y, ua, delta_s = _mixer_sample(proj, gb, bufa[l], bufq[l], caw[l], canw, cqw[l], dnw, state_delta, delta_s,
                                       layer=l)
        xs, w_out16 = _outproj(y, w_out, xs, layer=l, tm=tm_s)
        xs, w_up16, w_down16 = _ffn(xs, nfw, w_up, w_down, fw, layer=l, tm=tm_s, final=final)
        conv_a_s.append(ua.reshape(n_dec, SAMPLE_LEN, D_CONV)[:, SAMPLE_LEN - 2:])
        conv_q_s.append(proj[:, QKV_OFF:QKV_OFF + 3 * D_DN].reshape(n_dec, SAMPLE_LEN, 3 * D_DN)[:, SAMPLE_LEN - 3:])
        proj, gb, _ = _proj(xp, nmw, w_main16, w_in_t, alog[l], dtb[l], layer=l, tm=tm_p, tn=1024)
        y, last_a, last_q, s_fin = _mixer_prompt(proj, gb, caw[l], canw, cqw[l], dnw, n_seq=n_seq, seq_len=seq_len)
        xp, _ = _outproj(y, w_out16, xp, layer=l, tm=tm_p, tn=1024)
        xp, _, _ = _ffn(xp, nfw, w_up16, w_down16, fw, layer=l, tm=tm_p, final=final)
        conv_a_p.append(last_a[:, SUBLANES - 2:])
        conv_q_p.append(last_q[:, SUBLANES - 3:])
        delta_p.append(s_fin)

    return (xp.reshape(n_seq, seq_len, D_MODEL), xs.reshape(n_dec, dec_len, D_MODEL),
            jnp.stack(conv_a_p), jnp.stack(conv_q_p), jnp.stack(delta_p),
            jnp.stack(conv_a_s), jnp.stack(conv_q_s), delta_s)
```

```python
import functools

import jax
import jax.numpy as jnp
from jax import lax
from jax.experimental import pallas as pl
from jax.experimental.pallas import tpu as pltpu

F32 = jnp.float32
BF16 = jnp.bfloat16

D_MODEL = 2048
D_CONV = 1024
D_DN = 1024
N_HEADS = 8
DK = 128
D_FF = 8192
D_MAIN = 3 * D_CONV + 4 * D_DN
QKV_OFF = 3 * D_CONV
Z_OFF = QKV_OFF + 3 * D_DN
EPS = 1e-6
LANES = 128
SUBLANES = 8
CHUNK = 128
SAMPLE_LEN = 4
VMEM_LIMIT = 56 * 1024 * 1024


def _dot(a, b):
    return jnp.dot(a, b, preferred_element_type=F32)


def _dot_nt(a, b):
    return lax.dot_general(a, b, (((1,), (1,)), ((), ())), preferred_element_type=F32)


def _dot_tn(a, b):
    return lax.dot_general(a, b, (((0,), (0,)), ((), ())), preferred_element_type=F32)


def _softplus(x):
    return jnp.maximum(x, 0.0) + jnp.log1p(jnp.exp(-jnp.abs(x)))


def _silu(x):
    return x * jax.nn.sigmoid(x)


def _rms(x):
    return x * lax.rsqrt(jnp.mean(x * x, axis=-1, keepdims=True) + EPS)


def _weight_spec(w, layer, block, index):
    if w.ndim == 2:
        return pl.BlockSpec(block, index)
    return pl.BlockSpec((None,) + block, lambda *g: (layer,) + index(*g))


def _bf16_weight_ref(w_ref, copy_ref):
    if copy_ref is None:
        return w_ref
    copy_ref[...] = w_ref[...].astype(BF16)
    return copy_ref


def _proj_kernel(x_ref, nw_ref, w_ref, wab_ref, alog_ref, dtb_ref, proj_ref, gb_ref, *rest, tm, emit):
    w16_ref, h_scr, wab_scr = rest if emit else (None,) + rest
    w = _bf16_weight_ref(w_ref, w16_ref)
    first = pl.program_id(1) == 0

    @pl.when(first)
    def _():
        wab_scr[...] = jnp.zeros_like(wab_scr)
        wab_scr[:2 * N_HEADS, :] = wab_ref[...].astype(BF16)
        for r in range(0, tm, 256):
            rows = slice(r, r + 256)
            h = (_rms(x_ref[rows, :]) * nw_ref[...]).astype(BF16)
            h_scr[rows, :] = h
            ab = _dot_nt(h, wab_scr[...])
            lane = lax.broadcasted_iota(jnp.int32, ab.shape, 1)
            g = -jnp.exp(alog_ref[...]) * _softplus(ab + dtb_ref[...])
            gb_ref[rows, :] = jnp.where(lane < N_HEADS, g, jax.nn.sigmoid(ab))
            proj_ref[rows, :] = _dot_nt(h, w[...])

    @pl.when(jnp.logical_not(first))
    def _():
        proj_ref[...] = _dot_nt(h_scr[...], w[...])


def _emits_bf16(w, m, tm):
    emit = w.dtype == F32
    assert not emit or m == tm
    return emit


def _proj(x, nw, w, w_in_t, alog, dtb, *, layer, tm, tn=512):
    m = x.shape[0]
    emit = _emits_bf16(w, m, tm)
    n_gate = 2 * N_HEADS
    out_shape = [jax.ShapeDtypeStruct((m, D_MAIN), F32), jax.ShapeDtypeStruct((m, LANES), F32)]
    out_specs = [pl.BlockSpec((tm, tn), lambda i, j: (i, j)), pl.BlockSpec((tm, LANES), lambda i, j: (i, 0))]
    if emit:
        out_shape.append(jax.ShapeDtypeStruct((D_MAIN, D_MODEL), BF16))
        out_specs.append(pl.BlockSpec((tn, D_MODEL), lambda i, j: (j, 0)))
    outs = pl.pallas_call(
        functools.partial(_proj_kernel, tm=tm, emit=emit),
        out_shape=tuple(out_shape),
        grid=(m // tm, D_MAIN // tn),
        in_specs=[
            pl.BlockSpec((tm, D_MODEL), lambda i, j: (i, 0)),
            pl.BlockSpec((1, D_MODEL), lambda i, j: (0, 0)),
            _weight_spec(w, layer, (tn, D_MODEL), lambda i, j: (j, 0)),
            pl.BlockSpec((None, n_gate, D_MODEL), lambda i, j: (layer, D_MAIN // n_gate, 0)),
            pl.BlockSpec((1, LANES), lambda i, j: (0, 0)),
            pl.BlockSpec((1, LANES), lambda i, j: (0, 0)),
        ],
        out_specs=tuple(out_specs),
        scratch_shapes=[pltpu.VMEM((tm, D_MODEL), BF16), pltpu.VMEM((LANES, D_MODEL), BF16)],
        compiler_params=pltpu.CompilerParams(
            dimension_semantics=("arbitrary", "arbitrary"), vmem_limit_bytes=VMEM_LIMIT),
        name="norm_proj",
    )(x, nw, w, w_in_t, alog, dtb)
    return outs if emit else (*outs, w)


def _outproj_kernel(y_ref, w_ref, x_ref, o_ref, w16_ref=None):
    o_ref[...] = x_ref[...] + _dot(y_ref[...], _bf16_weight_ref(w_ref, w16_ref)[...])


def _outproj(y, w, x, *, layer, tm, tn=512):
    m = x.shape[0]
    emit = _emits_bf16(w, m, tm)
    out_shape = [jax.ShapeDtypeStruct((m, D_MODEL), F32)]
    out_specs = [pl.BlockSpec((tm, tn), lambda i, j: (i, j))]
    if emit:
        out_shape.append(jax.ShapeDtypeStruct((D_MODEL, D_MODEL), BF16))
        out_specs.append(pl.BlockSpec((D_MODEL, tn), lambda i, j: (0, j)))
    outs = pl.pallas_call(
        _outproj_kernel,
        out_shape=tuple(out_shape),
        grid=(m // tm, D_MODEL // tn),
        in_specs=[
            pl.BlockSpec((tm, D_MODEL), lambda i, j: (i, 0)),
            _weight_spec(w, layer, (D_MODEL, tn), lambda i, j: (0, j)),
            pl.BlockSpec((tm, tn), lambda i, j: (i, j)),
        ],
        out_specs=tuple(out_specs),
        compiler_params=pltpu.CompilerParams(
            dimension_semantics=("arbitrary", "arbitrary"), vmem_limit_bytes=VMEM_LIMIT),
        name="out_proj",
    )(y, w, x)
    return outs if emit else (*outs, w)


def _ffn_kernel(x_ref, nw_ref, wu_ref, wd_ref, fw_ref, o_ref, *rest, tm, final, emit):
    wu16_ref, wd16_ref, h_scr = rest if emit else (None, None) + rest
    f = pl.program_id(1)
    wu = _bf16_weight_ref(wu_ref, wu16_ref)
    wd = _bf16_weight_ref(wd_ref, wd16_ref)

    def mlp(h):
        up = jnp.maximum(_dot(h, wu[...]), 0.0)
        return _dot((up * up).astype(BF16), wd[...])

    @pl.when(f == 0)
    def _():
        for r in range(0, tm, 256):
            rows = slice(r, r + 256)
            x = x_ref[rows, :]
            h = (_rms(x) * nw_ref[...]).astype(BF16)
            h_scr[rows, :] = h
            o_ref[rows, :] = x + mlp(h)

    @pl.when(f > 0)
    def _():
        o_ref[...] += mlp(h_scr[...])

    if final:
        @pl.when(f == pl.num_programs(1) - 1)
        def _():
            for r in range(0, tm, 256):
                rows = slice(r, r + 256)
                o_ref[rows, :] = _rms(o_ref[rows, :]) * fw_ref[...]


def _ffn(x, nw, wu, wd, fw, *, layer, tm, final, tf=512):
    m = x.shape[0]
    emit = _emits_bf16(wu, m, tm)
    assert (wd.dtype == F32) == emit
    out_shape = [jax.ShapeDtypeStruct((m, D_MODEL), F32)]
    out_specs = [pl.BlockSpec((tm, D_MODEL), lambda i, f: (i, 0))]
    if emit:
        out_shape += [jax.ShapeDtypeStruct((D_MODEL, D_FF), BF16), jax.ShapeDtypeStruct((D_FF, D_MODEL), BF16)]
        out_specs += [pl.BlockSpec((D_MODEL, tf), lambda i, f: (0, f)), pl.BlockSpec((tf, D_MODEL), lambda i, f: (f, 0))]
    outs = pl.pallas_call(
        functools.partial(_ffn_kernel, tm=tm, final=final, emit=emit),
        out_shape=tuple(out_shape),
        grid=(m // tm, D_FF // tf),
        in_specs=[
            pl.BlockSpec((tm, D_MODEL), lambda i, f: (i, 0)),
            pl.BlockSpec((1, D_MODEL), lambda i, f: (0, 0)),
            _weight_spec(wu, layer, (D_MODEL, tf), lambda i, f: (0, f)),
            _weight_spec(wd, layer, (tf, D_MODEL), lambda i, f: (f, 0)),
            pl.BlockSpec((1, D_MODEL), lambda i, f: (0, 0)),
        ],
        out_specs=tuple(out_specs),
        scratch_shapes=[pltpu.VMEM((tm, D_MODEL), BF16)],
        compiler_params=pltpu.CompilerParams(
            dimension_semantics=("arbitrary", "arbitrary"), vmem_limit_bytes=VMEM_LIMIT),
        name="ffn",
    )(x, nw, wu, wd, fw)
    return outs if emit else (*outs, wu, wd)


def _group_masks(n, group):
    row = lax.broadcasted_iota(jnp.int32, (n, n), 0)
    col = lax.broadcasted_iota(jnp.int32, (n, n), 1)
    same = (row // group) == (col // group)
    incl = same & (row >= col)
    strict = same & (row > col)
    eye = jnp.where(row == col, 1.0, 0.0).astype(F32)
    return row, col, incl, strict, eye


def _merge_masks(row, col, base, group):
    masks = []
    b = base
    while b < group:
        masks.append(((row // (2 * b)) == (col // (2 * b))) & ((row // b) != (col // b)))
        b *= 2
    return masks


def _each(f, *lists):
    return [f(*args) for args in zip(*lists)]


def _unit_lower_inverse(a, eye, base, base_mask, merge_masks):
    n1 = a if base_mask is None else _each(lambda t: jnp.where(base_mask, t, 0.0), a)
    n1b = _each(lambda t: t.astype(BF16), n1)
    n2 = _each(_dot, n1b, n1b)
    n2b = _each(lambda t: t.astype(BF16), n2)
    n3 = _each(_dot, n1b, n2b)
    x = _each(lambda p1, p2, p3: eye - p1 + p2 - p3, n1, n2, n3)
    if base == 8:
        n4b = _each(lambda t: _dot(t, t).astype(BF16), n2b)
        x = _each(lambda t, p4: t + _dot(t.astype(BF16), p4), x, n4b)
    for m in merge_masks:
        xb = _each(lambda t: t.astype(BF16), x)
        xl = _each(lambda tb, t: _dot(tb, jnp.where(m, t, 0.0).astype(BF16)).astype(BF16), xb, a)
        x = _each(lambda t, l, tb: t - _dot(l, tb), x, xl, xb)
    return x


def _chunk_prepare(q, k, v, gcol, bcol, grow, incl, strict, inverse):
    decay = _each(lambda gc, gr: jnp.exp(jnp.where(incl, gc - gr, -jnp.inf)), gcol, grow)
    kb = _each(lambda t: t.astype(BF16), k)
    kk = _each(_dot_nt, kb, kb)
    qk = _each(lambda t, tb: _dot_nt(t.astype(BF16), tb), q, kb)
    a = _each(lambda b, d, m: jnp.where(strict, b * d * m, 0.0), bcol, decay, kk)
    t = inverse(a)
    eg = _each(jnp.exp, gcol)
    rhs = _each(lambda b, e, vv, kx: jnp.concatenate([b * vv, (b * e) * kx], axis=1).astype(BF16), bcol, eg, v, k)
    sol = _each(lambda tt, r: _dot(tt.astype(BF16), r), t, rhs)
    return (_each(lambda s: s[:, :DK], sol), _each(lambda s: s[:, DK:], sol),
            _each(lambda qq, e: qq * e, q, eg), _each(lambda m, d: m * d, qk, decay))


def _gated_out(o, z, dnw):
    return (_rms(o) * dnw * _silu(z)).astype(BF16)


def _conv_a_group(p_ref, gi, caw_ref, canw_ref, shift):
    cols = slice(gi * LANES, (gi + 1) * LANES)
    c_cols = slice(D_CONV + gi * LANES, D_CONV + (gi + 1) * LANES)
    h_cols = slice(2 * D_CONV + gi * LANES, 2 * D_CONV + (gi + 1) * LANES)
    b_a = p_ref[:, cols]
    u = p_ref[:, c_cols] * p_ref[:, h_cols]
    conv = (caw_ref[2:3, cols] * u + caw_ref[1:2, cols] * shift(u, 1, cols)
            + caw_ref[0:1, cols] * shift(u, 2, cols))
    y = b_a * conv
    return u, (_rms(y) * canw_ref[:, cols]).astype(BF16)


def _conv_qkv_group(p_ref, gi, cqw_ref, shift):
    cols = slice(gi * LANES, (gi + 1) * LANES)
    x_cols = slice(QKV_OFF + gi * LANES, QKV_OFF + (gi + 1) * LANES)
    x = p_ref[:, x_cols]
    conv = (cqw_ref[3:4, cols] * x + cqw_ref[2:3, cols] * shift(x, 1, cols)
            + cqw_ref[1:2, cols] * shift(x, 2, cols) + cqw_ref[0:1, cols] * shift(x, 3, cols))
    c = _silu(conv)
    if gi < 2 * N_HEADS:
        c = c * lax.rsqrt(jnp.sum(c * c, axis=-1, keepdims=True) + EPS)
        if gi < N_HEADS:
            c = c * (DK ** -0.5)
    return x, c


def _mixer_prompt_kernel(p_ref, gb_ref, caw_ref, canw_ref, cqw_ref, dnw_ref,
                         y_ref, lasta_ref, lastq_ref, sfin_ref,
                         qkv_scr, ua_carry, xq_carry, st_scr, *, rb):
    nb = pl.program_id(1)

    @pl.when(nb == 0)
    def _():
        ua_carry[...] = jnp.zeros_like(ua_carry)
        xq_carry[...] = jnp.zeros_like(xq_carry)
        st_scr[...] = jnp.zeros_like(st_scr)

    row8 = lax.broadcasted_iota(jnp.int32, (SUBLANES, LANES), 0)

    def make_shift(carry_ref):
        def shift(x, s, cols):
            rolled = pltpu.roll(x, s, 0)
            prev = pltpu.roll(carry_ref[:, cols], s, 0)
            first = jnp.where(row8 < s, prev, rolled[:SUBLANES])
            return jnp.concatenate([first, rolled[SUBLANES:]], axis=0)
        return shift

    shift_a = make_shift(ua_carry)
    for gi in range(D_CONV // LANES):
        cols = slice(gi * LANES, (gi + 1) * LANES)
        u, y = _conv_a_group(p_ref, gi, caw_ref, canw_ref, shift_a)
        ua_carry[:, cols] = u[rb - SUBLANES:]
        y_ref[:, cols] = y

    shift_q = make_shift(xq_carry)
    for gi in range(3 * D_DN // LANES):
        cols = slice(gi * LANES, (gi + 1) * LANES)
        x, c = _conv_qkv_group(p_ref, gi, cqw_ref, shift_q)
        xq_carry[:, cols] = x[rb - SUBLANES:]
        qkv_scr[:, cols] = c

    row, col, incl, strict, eye = _group_masks(CHUNK, CHUNK)
    base_mask = (row // 8) == (col // 8)
    merges = _merge_masks(row, col, 8, CHUNK)
    inverse = functools.partial(_unit_lower_inverse, eye=eye, base=8, base_mask=base_mask, merge_masks=merges)
    rowc = lax.broadcasted_iota(jnp.int32, (CHUNK, LANES), 0)

    heads = list(range(N_HEADS))
    chunks = list(range(rb // CHUNK))
    q, k, v, gcol, bcol, grow = [], [], [], [], [], []
    for c in chunks:
        rows = slice(c * CHUNK, (c + 1) * CHUNK)
        gb = gb_ref[rows, :]
        gcum = gb
        s = 1
        while s < CHUNK:
            gcum = gcum + jnp.where(rowc >= s, pltpu.roll(gcum, s, 0), 0.0)
            s *= 2
        gcum_t = gcum.T
        q += [qkv_scr[rows, h * DK:(h + 1) * DK] for h in heads]
        k += [qkv_scr[rows, D_DN + h * DK:D_DN + (h + 1) * DK] for h in heads]
        v += [qkv_scr[rows, 2 * D_DN + h * DK:2 * D_DN + (h + 1) * DK] for h in heads]
        gcol += [gcum[:, h:h + 1] for h in heads]
        bcol += [gb[:, N_HEADS + h:N_HEADS + h + 1] for h in heads]
        grow += [gcum_t[h:h + 1, :] for h in heads]
    ub, w, qd, p = _chunk_prepare(q, k, v, gcol, bcol, grow, incl, strict, inverse)
    ke = _each(lambda a, g: (a * jnp.exp(g[CHUNK - 1:CHUNK, :] - g)).astype(BF16), k, gcol)
    lhs = _each(lambda a, b: jnp.concatenate([a, b], axis=0).astype(BF16), w, qd)
    p16 = _each(lambda a: a.astype(BF16), p)
    g_end = _each(lambda g: jnp.exp(g[CHUNK - 1:CHUNK, :]), gcol)

    st = [st_scr[h] for h in heads]
    for c in chunks:
        rows = slice(c * CHUNK, (c + 1) * CHUNK)
        sl = slice(c * N_HEADS, (c + 1) * N_HEADS)
        ws = _each(lambda a, s: _dot(a, s.astype(BF16)), lhs[sl], st)
        u16 = _each(lambda a, b: (a - b[:CHUNK]).astype(BF16), ub[sl], ws)
        o = _each(lambda a, b, c: a[CHUNK:] + _dot(b, c), ws, p16[sl], u16)
        st = _each(lambda g, s, a, b: g * s + _dot_tn(a, b), g_end[sl], st, ke[sl], u16)
        for h in heads:
            y_ref[rows, D_CONV + h * DK:D_CONV + (h + 1) * DK] = _gated_out(
                o[h], p_ref[rows, Z_OFF + h * DK:Z_OFF + (h + 1) * DK], dnw_ref[...])
    for h in heads:
        st_scr[h] = st[h]

    @pl.when(nb == pl.num_programs(1) - 1)
    def _():
        lasta_ref[0] = ua_carry[...]
        lastq_ref[0] = xq_carry[...]
        for h in range(N_HEADS):
            sfin_ref[0, h] = st_scr[h].T


def _mixer_prompt(proj, gb, caw, canw, cqw, dnw, *, n_seq, seq_len, rb=256):
    nblk = seq_len // rb
    m = n_seq * seq_len
    full = lambda shape: pl.BlockSpec(shape, lambda b, n: (0,) * len(shape))
    return pl.pallas_call(
        functools.partial(_mixer_prompt_kernel, rb=rb),
        out_shape=(
            jax.ShapeDtypeStruct((m, D_MODEL), BF16),
            jax.ShapeDtypeStruct((n_seq, SUBLANES, D_CONV), F32),
            jax.ShapeDtypeStruct((n_seq, SUBLANES, 3 * D_DN), F32),
            jax.ShapeDtypeStruct((n_seq, N_HEADS, DK, DK), F32),
        ),
        grid=(n_seq, nblk),
        in_specs=[
            pl.BlockSpec((rb, D_MAIN), lambda b, n: (b * nblk + n, 0)),
            pl.BlockSpec((rb, LANES), lambda b, n: (b * nblk + n, 0)),
            full((SUBLANES, D_CONV)),
            full((1, D_CONV)),
            full((SUBLANES, 3 * D_DN)),
            full((1, DK)),
        ],
        out_specs=(
            pl.BlockSpec((rb, D_MODEL), lambda b, n: (b * nblk + n, 0)),
            pl.BlockSpec((1, SUBLANES, D_CONV), lambda b, n: (b, 0, 0)),
            pl.BlockSpec((1, SUBLANES, 3 * D_DN), lambda b, n: (b, 0, 0)),
            pl.BlockSpec((1, N_HEADS, DK, DK), lambda b, n: (b, 0, 0, 0)),
        ),
        scratch_shapes=[
            pltpu.VMEM((rb, 3 * D_DN), F32),
            pltpu.VMEM((SUBLANES, D_CONV), F32),
            pltpu.VMEM((SUBLANES, 3 * D_DN), F32),
            pltpu.VMEM((N_HEADS, DK, DK), F32),
        ],
        compiler_params=pltpu.CompilerParams(
            dimension_semantics=("arbitrary", "arbitrary"), vmem_limit_bytes=VMEM_LIMIT),
        name="mixer_prompt",
    )(proj, gb, caw, canw, cqw, dnw)


SEQ_PER_STEP = 8
ROWS_PER_STEP = SEQ_PER_STEP * SAMPLE_LEN


def _mixer_sample_kernel(p_ref, gb_ref, bufa_ref, bufq_ref, caw_ref, canw_ref, cqw_ref, dnw_ref, s_ref, prev_ref,
                         y_ref, newa_ref, newq_ref, snew_ref,
                         w_scr, qd_scr, ub_scr, ke_scr, p_scr, u_scr, qs_scr, g_scr, ua_scr, bxa_scr, xq_scr, bxq_scr):
    del prev_ref
    j = pl.program_id(1)
    n = CHUNK
    seqs = n // SAMPLE_LEN

    def token_rows(t):
        return pl.ds(t, seqs, stride=SAMPLE_LEN)

    @pl.when(j == 0)
    def _():
        rown = lax.broadcasted_iota(jnp.int32, (n, LANES), 0)
        tok = rown % SAMPLE_LEN

        def make_shift(buf_ref, bx_scr):
            first = SAMPLE_LEN - buf_ref.shape[0]
            bx_scr[...] = jnp.zeros_like(bx_scr)
            for m in range(buf_ref.shape[0]):
                for g in range(bx_scr.shape[0]):
                    bx_scr[g, token_rows(first + m), :] = buf_ref[m, :, g * LANES:(g + 1) * LANES]

            def shift(x, s, cols):
                return jnp.where(tok >= s, pltpu.roll(x, s, 0),
                                 pltpu.roll(bx_scr[cols.start // LANES], n - (SAMPLE_LEN - s), 0))
            return shift

        def emit_tail(raw_scr, new_ref):
            first = SAMPLE_LEN - new_ref.shape[0]
            for m in range(new_ref.shape[0]):
                for g in range(raw_scr.shape[0]):
                    new_ref[m, :, g * LANES:(g + 1) * LANES] = raw_scr[g, token_rows(first + m), :]

        shift_a = make_shift(bufa_ref, bxa_scr)
        for gi in range(D_CONV // LANES):
            cols = slice(gi * LANES, (gi + 1) * LANES)
            u, y = _conv_a_group(p_ref, gi, caw_ref, canw_ref, shift_a)
            ua_scr[gi] = u
            y_ref[:, cols] = y
        emit_tail(ua_scr, newa_ref)

        shift_q = make_shift(bufq_ref, bxq_scr)
        qkv = [None] * (3 * N_HEADS)
        for gi in range(3 * D_DN // LANES):
            xq_scr[gi], qkv[gi] = _conv_qkv_group(p_ref, gi, cqw_ref, shift_q)
        emit_tail(xq_scr, newq_ref)

        row, col, incl, strict, eye = _group_masks(n, SAMPLE_LEN)
        inverse = functools.partial(_unit_lower_inverse, eye=eye, base=SAMPLE_LEN, base_mask=None, merge_masks=[])

        gb = gb_ref[...]
        gcum = gb
        s = 1
        while s < SAMPLE_LEN:
            gcum = gcum + jnp.where(tok >= s, pltpu.roll(gcum, s, 0), 0.0)
            s *= 2
        glast = gcum
        for back in range(1, SAMPLE_LEN):
            glast = jnp.where(tok == SAMPLE_LEN - 1 - back, pltpu.roll(gcum, n - back, 0), glast)
        g_scr[...] = gcum
        gcum_t = gcum.T
        e_end = jnp.exp(glast - gcum)
        heads = list(range(N_HEADS))
        gcol = [gcum[:, h:h + 1] for h in heads]
        bcol = [gb[:, N_HEADS + h:N_HEADS + h + 1] for h in heads]
        grow = [gcum_t[h:h + 1, :] for h in heads]
        ub, w, qd, p = _chunk_prepare(qkv[:N_HEADS], qkv[N_HEADS:2 * N_HEADS], qkv[2 * N_HEADS:],
                                            gcol, bcol, grow, incl, strict, inverse)
        for h in heads:
            hc = slice(h * DK, (h + 1) * DK)
            ub_scr[:, hc] = ub[h]
            w_scr[:, hc] = w[h]
            qd_scr[:, hc] = qd[h]
            ke_scr[:, hc] = qkv[N_HEADS + h] * e_end[:, h:h + 1]
            p_scr[h] = p[h]
        u_scr[...] = jnp.zeros_like(u_scr)

    r0 = pl.multiple_of(j * ROWS_PER_STEP, ROWS_PER_STEP)
    row16 = lax.broadcasted_iota(jnp.int32, (2 * SUBLANES, LANES), 0)
    first_of_pair = (row16 % SUBLANES) < SAMPLE_LEN
    coln = lax.broadcasted_iota(jnp.int32, (DK, n), 1)
    heads = list(range(N_HEADS))
    hcs = [slice(h * DK, (h + 1) * DK) for h in heads]
    for h in heads:
        for t in range(ROWS_PER_STEP // SUBLANES):
            rows = pl.ds(pl.multiple_of(r0 + t * SUBLANES, SUBLANES), SUBLANES)
            lhs = jnp.concatenate([w_scr[rows, hcs[h]], qd_scr[rows, hcs[h]]], axis=0).astype(BF16)
            da = _dot_nt(lhs, s_ref[2 * t, h].astype(BF16))
            db = _dot_nt(lhs, s_ref[2 * t + 1, h].astype(BF16))
            ws = jnp.where(first_of_pair, da, db)
            u_scr[rows, hcs[h]] = ub_scr[rows, hcs[h]] - ws[:SUBLANES]
            qs_scr[rows, hcs[h]] = ws[SUBLANES:]
    u_h = [u_scr[:, hc] for hc in hcs]
    u_t = _each(lambda t: t.T.astype(BF16), u_h)
    ke16 = [ke_scr[:, hc].astype(BF16) for hc in hcs]
    for sq in range(SEQ_PER_STEP):
        seq_cols = jnp.where(coln // SAMPLE_LEN == j * SEQ_PER_STEP + sq, 1.0, 0.0).astype(BF16)
        g_end = jnp.exp(g_scr[pl.ds(r0 + sq * SAMPLE_LEN + SAMPLE_LEN - 1, 1), :])
        upd = _each(lambda t, ke: _dot(t * seq_cols, ke), u_t, ke16)
        for h in heads:
            snew_ref[sq, h] = g_end[:, h:h + 1] * s_ref[sq, h] + upd[h]
    rows = pl.ds(r0, ROWS_PER_STEP)
    o = [qs_scr[rows, hcs[h]] + _dot(p_scr[h, rows, :].astype(BF16), u_h[h].astype(BF16)) for h in heads]
    for h in heads:
        y_ref[rows, D_CONV + h * DK:D_CONV + (h + 1) * DK] = _gated_out(
            o[h], p_ref[rows, Z_OFF + h * DK:Z_OFF + (h + 1) * DK], dnw_ref[...])


def _mixer_sample(proj, gb, bufa, bufq, caw, canw, cqw, dnw, state, new_state, *, layer):
    m = proj.shape[0]
    n = CHUNK
    seqs = n // SAMPLE_LEN
    n_seq = m // SAMPLE_LEN
    steps = n // ROWS_PER_STEP
    full = lambda shape: pl.BlockSpec(shape, lambda i, j: (0,) * len(shape))
    rowblk = lambda width: pl.BlockSpec((n, width), lambda i, j: (i, 0))
    sblk = pl.BlockSpec((None, SEQ_PER_STEP, N_HEADS, DK, DK), lambda i, j: (layer, i * steps + j, 0, 0, 0))
    la, lq = bufa.shape[1], bufq.shape[1]
    prev_index = 9
    return pl.pallas_call(
        _mixer_sample_kernel,
        out_shape=(
            jax.ShapeDtypeStruct((m, D_MODEL), BF16),
            jax.ShapeDtypeStruct((la, n_seq, D_CONV), F32),
            jax.ShapeDtypeStruct((lq, n_seq, 3 * D_DN), F32),
            jax.ShapeDtypeStruct(state.shape, F32),
        ),
        grid=(m // n, steps),
        in_specs=[
            rowblk(D_MAIN), rowblk(LANES),
            pl.BlockSpec((None, la, seqs, D_CONV), lambda i, j: (layer, 0, i, 0)),
            pl.BlockSpec((None, lq, seqs, 3 * D_DN), lambda i, j: (layer, 0, i, 0)),
            full((SUBLANES, D_CONV)), full((1, D_CONV)), full((SUBLANES, 3 * D_DN)), full((1, DK)),
            sblk, pl.BlockSpec(memory_space=pl.ANY),
        ],
        input_output_aliases={prev_index: 3} if layer > 0 else {},
        out_specs=(rowblk(D_MODEL),
                   pl.BlockSpec((la, seqs, D_CONV), lambda i, j: (0, i, 0)),
                   pl.BlockSpec((lq, seqs, 3 * D_DN), lambda i, j: (0, i, 0)),
                   sblk),
        scratch_shapes=[
            pltpu.VMEM((n, D_DN), F32),
            pltpu.VMEM((n, D_DN), F32),
            pltpu.VMEM((n, D_DN), F32),
            pltpu.VMEM((n, D_DN), F32),
            pltpu.VMEM((N_HEADS, n, n), F32),
            pltpu.VMEM((n, D_DN), F32),
            pltpu.VMEM((n, D_DN), F32),
            pltpu.VMEM((n, LANES), F32),
            pltpu.VMEM((D_CONV // LANES, n, LANES), F32),
            pltpu.VMEM((D_CONV // LANES, n, LANES), F32),
            pltpu.VMEM((3 * D_DN // LANES, n, LANES), F32),
            pltpu.VMEM((3 * D_DN // LANES, n, LANES), F32),
        ],
        compiler_params=pltpu.CompilerParams(
            dimension_semantics=("arbitrary", "arbitrary"), vmem_limit_bytes=VMEM_LIMIT),
        name="mixer_sample",
    )(proj, gb, bufa, bufq, caw, canw, cqw, dnw, state, new_state)


def _pad_rows(w):
    return jnp.pad(w, ((0, 0), (0, SUBLANES - w.shape[1]), (0, 0)))


def kernel(x_prompt, x_sample, state_conv_a, state_conv_qkv, state_delta, norm_mix_w, w_in,
           conv_a_w, conv_a_norm_w, conv_qkv_w, a_log, dt_bias, dn_norm_w, w_out,
           norm_ffn_w, w_up, w_down, final_norm_w):
    depth = w_in.shape[0]
    n_seq, seq_len, _ = x_prompt.shape
    n_dec, dec_len, _ = x_sample.shape
    assert dec_len == SAMPLE_LEN

    w_in_t = jnp.swapaxes(w_in, 1, 2)
    alog = jnp.pad(a_log, ((0, 0), (0, LANES - N_HEADS)))[:, None, :]
    dtb = jnp.pad(dt_bias, ((0, 0), (0, LANES - N_HEADS)))[:, None, :]
    caw, cqw = _pad_rows(conv_a_w), _pad_rows(conv_qkv_w)
    bufa = jnp.swapaxes(state_conv_a, 1, 2)
    bufq = jnp.swapaxes(state_conv_qkv, 1, 2)

    xp = x_prompt.reshape(n_seq * seq_len, D_MODEL)
    xs = x_sample.reshape(n_dec * dec_len, D_MODEL)
    tm_p, tm_s = 1024, 512
    fw = final_norm_w[None, :]
    conv_a_p, conv_q_p, delta_p, conv_a_s, conv_q_s = [], [], [], [], []
    delta_s = state_delta
    for l in range(depth):
        nmw, nfw = norm_mix_w[l][None, :], norm_ffn_w[l][None, :]
        canw, dnw = conv_a_norm_w[l][None, :], dn_norm_w[l][None, :]
        final = l == depth - 1
        proj, gb, w_main16 = _proj(xs, nmw, w_in_t, w_in_t, alog[l], dtb[l], layer=l, tm=tm_s)
        y, new_a, new_q, delta_s = _mixer_sample(proj, gb, bufa, bufq, caw[l], canw, cqw[l], dnw, state_delta,
                                                 delta_s, layer=l)
        xs, w_out16 = _outproj(y, w_out, xs, layer=l, tm=tm_s)
        xs, w_up16, w_down16 = _ffn(xs, nfw, w_up, w_down, fw, layer=l, tm=tm_s, final=final)
        conv_a_s.append(new_a)
        conv_q_s.append(new_q)
        proj, gb, _ = _proj(xp, nmw, w_main16, w_in_t, alog[l], dtb[l], layer=l, tm=tm_p, tn=1024)
        y, last_a, last_q, s_fin = _mixer_prompt(proj, gb, caw[l], canw, cqw[l], dnw, n_seq=n_seq, seq_len=seq_len)
        xp, _ = _outproj(y, w_out16, xp, layer=l, tm=tm_p, tn=1024)
        xp, _, _ = _ffn(xp, nfw, w_up16, w_down16, fw, layer=l, tm=tm_p, final=final)
        conv_a_p.append(last_a[:, SUBLANES - 2:])
        conv_q_p.append(last_q[:, SUBLANES - 3:])
        delta_p.append(s_fin)

    return (xp.reshape(n_seq, seq_len, D_MODEL), xs.reshape(n_dec, dec_len, D_MODEL),
            jnp.stack(conv_a_p), jnp.stack(conv_q_p), jnp.stack(delta_p),
            jnp.swapaxes(jnp.stack(conv_a_s), 1, 2), jnp.swapaxes(jnp.stack(conv_q_s), 1, 2), delta_s)
```

```python
import functools

import jax
import jax.numpy as jnp
from jax import lax
from jax.experimental import pallas as pl
from jax.experimental.pallas import tpu as pltpu

F32 = jnp.float32
BF16 = jnp.bfloat16

D_MODEL = 2048
D_CONV = 1024
D_DN = 1024
N_HEADS = 8
DK = 128
D_FF = 8192
D_MAIN = 3 * D_CONV + 4 * D_DN
QKV_OFF = 3 * D_CONV
Z_OFF = QKV_OFF + 3 * D_DN
EPS = 1e-6
LANES = 128
SUBLANES = 8
CHUNK = 128
SAMPLE_LEN = 4
VMEM_LIMIT = 56 * 1024 * 1024


def _dot(a, b):
    return jnp.dot(a, b, preferred_element_type=F32)


def _dot_nt(a, b):
    return lax.dot_general(a, b, (((1,), (1,)), ((), ())), preferred_element_type=F32)


def _dot_tn(a, b):
    return lax.dot_general(a, b, (((0,), (0,)), ((), ())), preferred_element_type=F32)


def _softplus(x):
    return jnp.maximum(x, 0.0) + jnp.log1p(jnp.exp(-jnp.abs(x)))


def _silu(x):
    return x * jax.nn.sigmoid(x)


def _rms(x):
    return x * lax.rsqrt(jnp.mean(x * x, axis=-1, keepdims=True) + EPS)


def _weight_spec(w, layer, block, index):
    if w.ndim == 2:
        return pl.BlockSpec(block, index)
    return pl.BlockSpec((None,) + block, lambda *g: (layer,) + index(*g))


def _bf16_weight_ref(w_ref, copy_ref):
    if copy_ref is None:
        return w_ref
    copy_ref[...] = w_ref[...].astype(BF16)
    return copy_ref


def _proj_kernel(x_ref, nw_ref, w_ref, wab_ref, alog_ref, dtb_ref, proj_ref, gb_ref, *rest, tm, emit):
    w16_ref, h_scr, wab_scr = rest if emit else (None,) + rest
    w = _bf16_weight_ref(w_ref, w16_ref)
    first = pl.program_id(1) == 0

    @pl.when(first)
    def _():
        wab_scr[...] = jnp.zeros_like(wab_scr)
        wab_scr[:2 * N_HEADS, :] = wab_ref[...].astype(BF16)
        for r in range(0, tm, 256):
            rows = slice(r, r + 256)
            h = (_rms(x_ref[rows, :]) * nw_ref[...]).astype(BF16)
            h_scr[rows, :] = h
            ab = _dot_nt(h, wab_scr[...])
            lane = lax.broadcasted_iota(jnp.int32, ab.shape, 1)
            g = -jnp.exp(alog_ref[...]) * _softplus(ab + dtb_ref[...])
            gb_ref[rows, :] = jnp.where(lane < N_HEADS, g, jax.nn.sigmoid(ab))
            proj_ref[rows, :] = _dot_nt(h, w[...])

    @pl.when(jnp.logical_not(first))
    def _():
        proj_ref[...] = _dot_nt(h_scr[...], w[...])


def _emits_bf16(w, m, tm):
    emit = w.dtype == F32
    assert not emit or m == tm
    return emit


def _proj(x, nw, w, w_in_t, alog, dtb, *, layer, tm, tn=512):
    m = x.shape[0]
    emit = _emits_bf16(w, m, tm)
    n_gate = 2 * N_HEADS
    out_shape = [jax.ShapeDtypeStruct((m, D_MAIN), F32), jax.ShapeDtypeStruct((m, LANES), F32)]
    out_specs = [pl.BlockSpec((tm, tn), lambda i, j: (i, j)), pl.BlockSpec((tm, LANES), lambda i, j: (i, 0))]
    if emit:
        out_shape.append(jax.ShapeDtypeStruct((D_MAIN, D_MODEL), BF16))
        out_specs.append(pl.BlockSpec((tn, D_MODEL), lambda i, j: (j, 0)))
    outs = pl.pallas_call(
        functools.partial(_proj_kernel, tm=tm, emit=emit),
        out_shape=tuple(out_shape),
        grid=(m // tm, D_MAIN // tn),
        in_specs=[
            pl.BlockSpec((tm, D_MODEL), lambda i, j: (i, 0)),
            pl.BlockSpec((1, D_MODEL), lambda i, j: (0, 0)),
            _weight_spec(w, layer, (tn, D_MODEL), lambda i, j: (j, 0)),
            pl.BlockSpec((None, n_gate, D_MODEL), lambda i, j: (layer, D_MAIN // n_gate, 0)),
            pl.BlockSpec((1, LANES), lambda i, j: (0, 0)),
            pl.BlockSpec((1, LANES), lambda i, j: (0, 0)),
        ],
        out_specs=tuple(out_specs),
        scratch_shapes=[pltpu.VMEM((tm, D_MODEL), BF16), pltpu.VMEM((LANES, D_MODEL), BF16)],
        compiler_params=pltpu.CompilerParams(
            dimension_semantics=("arbitrary", "arbitrary"), vmem_limit_bytes=VMEM_LIMIT),
        name="norm_proj",
    )(x, nw, w, w_in_t, alog, dtb)
    return outs if emit else (*outs, w)


QKV_KINDS = ("q", "k", "v")


def _conv_qkv_value(x, shifted, w, kind):
    conv = w[3:4] * x + w[2:3] * shifted(1) + w[1:2] * shifted(2) + w[0:1] * shifted(3)
    c = _silu(conv)
    if kind != "v":
        c = c * lax.rsqrt(jnp.sum(c * c, axis=-1, keepdims=True) + EPS)
        if kind == "q":
            c = c * (DK ** -0.5)
    return c


def _proj_conv_kernel(x_ref, nw_ref, w_ref, wab_ref, alog_ref, dtb_ref, cqw_ref, proj_ref, gb_ref, lastq_ref,
                      h_scr, wab_scr, carry_scr, *, tm, tn, blocks_per_seq):
    i, j = pl.program_id(0), pl.program_id(1)
    first_qkv = QKV_OFF // tn
    sub = 256
    row8 = lax.broadcasted_iota(jnp.int32, (SUBLANES, LANES), 0)

    @pl.when(j == 0)
    def _():
        wab_scr[...] = jnp.zeros_like(wab_scr)
        wab_scr[:2 * N_HEADS, :] = wab_ref[...].astype(BF16)
        for r in range(0, tm, sub):
            rows = slice(r, r + sub)
            h = (_rms(x_ref[rows, :]) * nw_ref[...]).astype(BF16)
            h_scr[rows, :] = h
            ab = _dot_nt(h, wab_scr[...])
            lane = lax.broadcasted_iota(jnp.int32, ab.shape, 1)
            g = -jnp.exp(alog_ref[...]) * _softplus(ab + dtb_ref[...])
            gb_ref[rows, :] = jnp.where(lane < N_HEADS, g, jax.nn.sigmoid(ab))
            proj_ref[rows, :] = _dot_nt(h, w_ref[...])

    @pl.when((j > 0) & ((j < first_qkv) | (j >= first_qkv + len(QKV_KINDS))))
    def _():
        proj_ref[...] = _dot_nt(h_scr[...], w_ref[...])

    def qkv_tile(t, kind):
        prev_all = jnp.where(i % blocks_per_seq == 0, 0.0, carry_scr[t])
        prev = [prev_all[:, g * LANES:(g + 1) * LANES] for g in range(tn // LANES)]
        starts = list(range(0, tm, sub))
        matmul = lambda r: _dot_nt(h_scr[r:r + sub, :], w_ref[...])
        acc_next = matmul(starts[0])
        for r in starts:
            rows = slice(r, r + sub)
            acc, acc_next = acc_next, (matmul(r + sub) if r + sub < tm else None)
            for g in range(tn // LANES):
                cols = slice(g * LANES, (g + 1) * LANES)
                x = acc[:, cols]

                def shifted(s, x=x, above=prev[g]):
                    rolled = pltpu.roll(x, s, 0)
                    head = jnp.where(row8 < s, pltpu.roll(above, s, 0), rolled[:SUBLANES])
                    return jnp.concatenate([head, rolled[SUBLANES:]], axis=0)

                proj_ref[rows, cols] = _conv_qkv_value(x, shifted, cqw_ref[:, cols], kind)
                prev[g] = x[sub - SUBLANES:]
        tail = jnp.concatenate(prev, axis=1)
        carry_scr[t] = tail
        lastq_ref[0] = tail

    for t, kind in enumerate(QKV_KINDS):
        pl.when(j == first_qkv + t)(functools.partial(qkv_tile, t, kind))


def _proj_conv(x, nw, w16, w_in_t, alog, dtb, cqw, *, layer, seq_len, tm, tn=D_DN):
    m = x.shape[0]
    assert tn == D_DN and QKV_OFF % tn == 0 and seq_len % tm == 0
    n_gate = 2 * N_HEADS
    blocks_per_seq = seq_len // tm
    first_qkv = QKV_OFF // tn
    qkv_tile_index = lambda j: jnp.clip(j - first_qkv, 0, len(QKV_KINDS) - 1)
    return pl.pallas_call(
        functools.partial(_proj_conv_kernel, tm=tm, tn=tn, blocks_per_seq=blocks_per_seq),
        out_shape=(jax.ShapeDtypeStruct((m, D_MAIN), F32), jax.ShapeDtypeStruct((m, LANES), F32),
                   jax.ShapeDtypeStruct((m // tm, SUBLANES, 3 * D_DN), F32)),
        grid=(m // tm, D_MAIN // tn),
        in_specs=[
            pl.BlockSpec((tm, D_MODEL), lambda i, j: (i, 0)),
            pl.BlockSpec((1, D_MODEL), lambda i, j: (0, 0)),
            pl.BlockSpec((tn, D_MODEL), lambda i, j: (j, 0)),
            pl.BlockSpec((None, n_gate, D_MODEL), lambda i, j: (layer, D_MAIN // n_gate, 0)),
            pl.BlockSpec((1, LANES), lambda i, j: (0, 0)),
            pl.BlockSpec((1, LANES), lambda i, j: (0, 0)),
            pl.BlockSpec((SUBLANES, tn), lambda i, j: (0, qkv_tile_index(j))),
        ],
        out_specs=(pl.BlockSpec((tm, tn), lambda i, j: (i, j)),
                   pl.BlockSpec((tm, LANES), lambda i, j: (i, 0)),
                   pl.BlockSpec((1, SUBLANES, tn), lambda i, j: (i, 0, qkv_tile_index(j)))),
        scratch_shapes=[pltpu.VMEM((tm, D_MODEL), BF16), pltpu.VMEM((LANES, D_MODEL), BF16),
                        pltpu.VMEM((len(QKV_KINDS), SUBLANES, tn), F32)],
        compiler_params=pltpu.CompilerParams(
            dimension_semantics=("arbitrary", "arbitrary"), vmem_limit_bytes=VMEM_LIMIT),
        name="norm_proj_conv",
    )(x, nw, w16, w_in_t, alog, dtb, cqw)


def _outproj_kernel(y_ref, w_ref, x_ref, o_ref, w16_ref=None):
    o_ref[...] = x_ref[...] + _dot(y_ref[...], _bf16_weight_ref(w_ref, w16_ref)[...])


def _outproj(y, w, x, *, layer, tm, tn=512):
    m = x.shape[0]
    emit = _emits_bf16(w, m, tm)
    out_shape = [jax.ShapeDtypeStruct((m, D_MODEL), F32)]
    out_specs = [pl.BlockSpec((tm, tn), lambda i, j: (i, j))]
    if emit:
        out_shape.append(jax.ShapeDtypeStruct((D_MODEL, D_MODEL), BF16))
        out_specs.append(pl.BlockSpec((D_MODEL, tn), lambda i, j: (0, j)))
    outs = pl.pallas_call(
        _outproj_kernel,
        out_shape=tuple(out_shape),
        grid=(m // tm, D_MODEL // tn),
        in_specs=[
            pl.BlockSpec((tm, D_MODEL), lambda i, j: (i, 0)),
            _weight_spec(w, layer, (D_MODEL, tn), lambda i, j: (0, j)),
            pl.BlockSpec((tm, tn), lambda i, j: (i, j)),
        ],
        out_specs=tuple(out_specs),
        compiler_params=pltpu.CompilerParams(
            dimension_semantics=("arbitrary", "arbitrary"), vmem_limit_bytes=VMEM_LIMIT),
        name="out_proj",
    )(y, w, x)
    return outs if emit else (*outs, w)


def _ffn_kernel(x_ref, nw_ref, wu_ref, wd_ref, fw_ref, o_ref, *rest, tm, final, emit):
    wu16_ref, wd16_ref, h_scr = rest if emit else (None, None) + rest
    f = pl.program_id(1)
    wu = _bf16_weight_ref(wu_ref, wu16_ref)
    wd = _bf16_weight_ref(wd_ref, wd16_ref)

    def mlp(h):
        up = jnp.maximum(_dot(h, wu[...]), 0.0)
        return _dot((up * up).astype(BF16), wd[...])

    @pl.when(f == 0)
    def _():
        for r in range(0, tm, 256):
            rows = slice(r, r + 256)
            x = x_ref[rows, :]
            h = (_rms(x) * nw_ref[...]).astype(BF16)
            h_scr[rows, :] = h
            o_ref[rows, :] = x + mlp(h)

    @pl.when(f > 0)
    def _():
        o_ref[...] += mlp(h_scr[...])

    if final:
        @pl.when(f == pl.num_programs(1) - 1)
        def _():
            for r in range(0, tm, 256):
                rows = slice(r, r + 256)
                o_ref[rows, :] = _rms(o_ref[rows, :]) * fw_ref[...]


def _ffn(x, nw, wu, wd, fw, *, layer, tm, final, tf=512):
    m = x.shape[0]
    emit = _emits_bf16(wu, m, tm)
    assert (wd.dtype == F32) == emit
    out_shape = [jax.ShapeDtypeStruct((m, D_MODEL), F32)]
    out_specs = [pl.BlockSpec((tm, D_MODEL), lambda i, f: (i, 0))]
    if emit:
        out_shape += [jax.ShapeDtypeStruct((D_MODEL, D_FF), BF16), jax.ShapeDtypeStruct((D_FF, D_MODEL), BF16)]
        out_specs += [pl.BlockSpec((D_MODEL, tf), lambda i, f: (0, f)), pl.BlockSpec((tf, D_MODEL), lambda i, f: (f, 0))]
    outs = pl.pallas_call(
        functools.partial(_ffn_kernel, tm=tm, final=final, emit=emit),
        out_shape=tuple(out_shape),
        grid=(m // tm, D_FF // tf),
        in_specs=[
            pl.BlockSpec((tm, D_MODEL), lambda i, f: (i, 0)),
            pl.BlockSpec((1, D_MODEL), lambda i, f: (0, 0)),
            _weight_spec(wu, layer, (D_MODEL, tf), lambda i, f: (0, f)),
            _weight_spec(wd, layer, (tf, D_MODEL), lambda i, f: (f, 0)),
            pl.BlockSpec((1, D_MODEL), lambda i, f: (0, 0)),
        ],
        out_specs=tuple(out_specs),
        scratch_shapes=[pltpu.VMEM((tm, D_MODEL), BF16)],
        compiler_params=pltpu.CompilerParams(
            dimension_semantics=("arbitrary", "arbitrary"), vmem_limit_bytes=VMEM_LIMIT),
        name="ffn",
    )(x, nw, wu, wd, fw)
    return outs if emit else (*outs, wu, wd)


def _group_masks(n, group):
    row = lax.broadcasted_iota(jnp.int32, (n, n), 0)
    col = lax.broadcasted_iota(jnp.int32, (n, n), 1)
    same = (row // group) == (col // group)
    incl = same & (row >= col)
    strict = same & (row > col)
    eye = jnp.where(row == col, 1.0, 0.0).astype(F32)
    return row, col, incl, strict, eye


def _merge_masks(row, col, base, group):
    masks = []
    b = base
    while b < group:
        masks.append(((row // (2 * b)) == (col // (2 * b))) & ((row // b) != (col // b)))
        b *= 2
    return masks


def _each(f, *lists):
    return [f(*args) for args in zip(*lists)]


def _unit_lower_inverse(a, eye, base, base_mask, merge_masks):
    n1 = a if base_mask is None else _each(lambda t: jnp.where(base_mask, t, 0.0), a)
    n1b = _each(lambda t: t.astype(BF16), n1)
    n2 = _each(_dot, n1b, n1b)
    n2b = _each(lambda t: t.astype(BF16), n2)
    n3 = _each(_dot, n1b, n2b)
    x = _each(lambda p1, p2, p3: eye - p1 + p2 - p3, n1, n2, n3)
    if base == 8:
        n4b = _each(lambda t: _dot(t, t).astype(BF16), n2b)
        x = _each(lambda t, p4: t + _dot(t.astype(BF16), p4), x, n4b)
    for m in merge_masks:
        xb = _each(lambda t: t.astype(BF16), x)
        xl = _each(lambda tb, t: _dot(tb, jnp.where(m, t, 0.0).astype(BF16)).astype(BF16), xb, a)
        x = _each(lambda t, l, tb: t - _dot(l, tb), x, xl, xb)
    return x


def _chunk_prepare(q, k, v, gcol, bcol, grow, incl, strict, inverse):
    decay = _each(lambda gc, gr: jnp.exp(jnp.where(incl, gc - gr, -jnp.inf)), gcol, grow)
    kb = _each(lambda t: t.astype(BF16), k)
    kk = _each(_dot_nt, kb, kb)
    qk = _each(lambda t, tb: _dot_nt(t.astype(BF16), tb), q, kb)
    a = _each(lambda b, d, m: jnp.where(strict, b * d * m, 0.0), bcol, decay, kk)
    t = inverse(a)
    eg = _each(jnp.exp, gcol)
    rhs = _each(lambda b, e, vv, kx: jnp.concatenate([b * vv, (b * e) * kx], axis=1).astype(BF16), bcol, eg, v, k)
    sol = _each(lambda tt, r: _dot(tt.astype(BF16), r), t, rhs)
    return (_each(lambda s: s[:, :DK], sol), _each(lambda s: s[:, DK:], sol),
            _each(lambda qq, e: qq * e, q, eg), _each(lambda m, d: m * d, qk, decay))


def _gated_out(o, z, dnw):
    return (_rms(o) * dnw * _silu(z)).astype(BF16)


def _conv_a_group(p_ref, gi, caw_ref, canw_ref, shift):
    cols = slice(gi * LANES, (gi + 1) * LANES)
    c_cols = slice(D_CONV + gi * LANES, D_CONV + (gi + 1) * LANES)
    h_cols = slice(2 * D_CONV + gi * LANES, 2 * D_CONV + (gi + 1) * LANES)
    b_a = p_ref[:, cols]
    u = p_ref[:, c_cols] * p_ref[:, h_cols]
    conv = (caw_ref[2:3, cols] * u + caw_ref[1:2, cols] * shift(u, 1, cols)
            + caw_ref[0:1, cols] * shift(u, 2, cols))
    y = b_a * conv
    return u, (_rms(y) * canw_ref[:, cols]).astype(BF16)


def _conv_qkv_group(p_ref, gi, cqw_ref, shift):
    cols = slice(gi * LANES, (gi + 1) * LANES)
    x_cols = slice(QKV_OFF + gi * LANES, QKV_OFF + (gi + 1) * LANES)
    x = p_ref[:, x_cols]
    return x, _conv_qkv_value(x, lambda s: shift(x, s, cols), cqw_ref[:, cols], QKV_KINDS[gi // N_HEADS])


def _mixer_prompt_kernel(p_ref, gb_ref, caw_ref, canw_ref, dnw_ref,
                         y_ref, lasta_ref, sfin_ref,
                         ua_carry, st_scr, *, rb):
    nb = pl.program_id(1)

    @pl.when(nb == 0)
    def _():
        ua_carry[...] = jnp.zeros_like(ua_carry)
        st_scr[...] = jnp.zeros_like(st_scr)

    row8 = lax.broadcasted_iota(jnp.int32, (SUBLANES, LANES), 0)

    def make_shift(carry_ref):
        def shift(x, s, cols):
            rolled = pltpu.roll(x, s, 0)
            prev = pltpu.roll(carry_ref[:, cols], s, 0)
            first = jnp.where(row8 < s, prev, rolled[:SUBLANES])
            return jnp.concatenate([first, rolled[SUBLANES:]], axis=0)
        return shift

    shift_a = make_shift(ua_carry)
    for gi in range(D_CONV // LANES):
        cols = slice(gi * LANES, (gi + 1) * LANES)
        u, y = _conv_a_group(p_ref, gi, caw_ref, canw_ref, shift_a)
        ua_carry[:, cols] = u[rb - SUBLANES:]
        y_ref[:, cols] = y

    row, col, incl, strict, eye = _group_masks(CHUNK, CHUNK)
    base_mask = (row // 8) == (col // 8)
    merges = _merge_masks(row, col, 8, CHUNK)
    inverse = functools.partial(_unit_lower_inverse, eye=eye, base=8, base_mask=base_mask, merge_masks=merges)
    rowc = lax.broadcasted_iota(jnp.int32, (CHUNK, LANES), 0)

    heads = list(range(N_HEADS))
    chunks = list(range(rb // CHUNK))
    q, k, v, gcol, bcol, grow = [], [], [], [], [], []
    for c in chunks:
        rows = slice(c * CHUNK, (c + 1) * CHUNK)
        gb = gb_ref[rows, :]
        gcum = gb
        s = 1
        while s < CHUNK:
            gcum = gcum + jnp.where(rowc >= s, pltpu.roll(gcum, s, 0), 0.0)
            s *= 2
        gcum_t = gcum.T
        q += [p_ref[rows, QKV_OFF + h * DK:QKV_OFF + (h + 1) * DK] for h in heads]
        k += [p_ref[rows, QKV_OFF + D_DN + h * DK:QKV_OFF + D_DN + (h + 1) * DK] for h in heads]
        v += [p_ref[rows, QKV_OFF + 2 * D_DN + h * DK:QKV_OFF + 2 * D_DN + (h + 1) * DK] for h in heads]
        gcol += [gcum[:, h:h + 1] for h in heads]
        bcol += [gb[:, N_HEADS + h:N_HEADS + h + 1] for h in heads]
        grow += [gcum_t[h:h + 1, :] for h in heads]
    ub, w, qd, p = _chunk_prepare(q, k, v, gcol, bcol, grow, incl, strict, inverse)
    ke = _each(lambda a, g: (a * jnp.exp(g[CHUNK - 1:CHUNK, :] - g)).astype(BF16), k, gcol)
    lhs = _each(lambda a, b: jnp.concatenate([a, b], axis=0).astype(BF16), w, qd)
    p16 = _each(lambda a: a.astype(BF16), p)
    g_end = _each(lambda g: jnp.exp(g[CHUNK - 1:CHUNK, :]), gcol)

    st = [st_scr[h] for h in heads]
    for c in chunks:
        rows = slice(c * CHUNK, (c + 1) * CHUNK)
        sl = slice(c * N_HEADS, (c + 1) * N_HEADS)
        ws = _each(lambda a, s: _dot(a, s.astype(BF16)), lhs[sl], st)
        u16 = _each(lambda a, b: (a - b[:CHUNK]).astype(BF16), ub[sl], ws)
        o = _each(lambda a, b, c: a[CHUNK:] + _dot(b, c), ws, p16[sl], u16)
        st = _each(lambda g, s, a, b: g * s + _dot_tn(a, b), g_end[sl], st, ke[sl], u16)
        for h in heads:
            y_ref[rows, D_CONV + h * DK:D_CONV + (h + 1) * DK] = _gated_out(
                o[h], p_ref[rows, Z_OFF + h * DK:Z_OFF + (h + 1) * DK], dnw_ref[...])
    for h in heads:
        st_scr[h] = st[h]

    @pl.when(nb == pl.num_programs(1) - 1)
    def _():
        lasta_ref[0] = ua_carry[...]
        for h in range(N_HEADS):
            sfin_ref[0, h] = st_scr[h].T


def _mixer_prompt(proj, gb, caw, canw, dnw, *, n_seq, seq_len, rb=256):
    nblk = seq_len // rb
    m = n_seq * seq_len
    full = lambda shape: pl.BlockSpec(shape, lambda b, n: (0,) * len(shape))
    return pl.pallas_call(
        functools.partial(_mixer_prompt_kernel, rb=rb),
        out_shape=(
            jax.ShapeDtypeStruct((m, D_MODEL), BF16),
            jax.ShapeDtypeStruct((n_seq, SUBLANES, D_CONV), F32),
            jax.ShapeDtypeStruct((n_seq, N_HEADS, DK, DK), F32),
        ),
        grid=(n_seq, nblk),
        in_specs=[
            pl.BlockSpec((rb, D_MAIN), lambda b, n: (b * nblk + n, 0)),
            pl.BlockSpec((rb, LANES), lambda b, n: (b * nblk + n, 0)),
            full((SUBLANES, D_CONV)),
            full((1, D_CONV)),
            full((1, DK)),
        ],
        out_specs=(
            pl.BlockSpec((rb, D_MODEL), lambda b, n: (b * nblk + n, 0)),
            pl.BlockSpec((1, SUBLANES, D_CONV), lambda b, n: (b, 0, 0)),
            pl.BlockSpec((1, N_HEADS, DK, DK), lambda b, n: (b, 0, 0, 0)),
        ),
        scratch_shapes=[
            pltpu.VMEM((SUBLANES, D_CONV), F32),
            pltpu.VMEM((N_HEADS, DK, DK), F32),
        ],
        compiler_params=pltpu.CompilerParams(
            dimension_semantics=("arbitrary", "arbitrary"), vmem_limit_bytes=VMEM_LIMIT),
        name="mixer_prompt",
    )(proj, gb, caw, canw, dnw)


SEQ_PER_STEP = 8
ROWS_PER_STEP = SEQ_PER_STEP * SAMPLE_LEN


def _mixer_sample_kernel(p_ref, gb_ref, bufa_ref, bufq_ref, caw_ref, canw_ref, cqw_ref, dnw_ref, s_ref, prev_ref,
                         y_ref, newa_ref, newq_ref, snew_ref,
                         w_scr, qd_scr, ub_scr, ke_scr, p_scr, u_scr, qs_scr, g_scr, ua_scr, bxa_scr, xq_scr, bxq_scr):
    del prev_ref
    j = pl.program_id(1)
    n = CHUNK
    seqs = n // SAMPLE_LEN

    def token_rows(t):
        return pl.ds(t, seqs, stride=SAMPLE_LEN)

    @pl.when(j == 0)
    def _():
        rown = lax.broadcasted_iota(jnp.int32, (n, LANES), 0)
        tok = rown % SAMPLE_LEN

        def make_shift(buf_ref, bx_scr):
            first = SAMPLE_LEN - buf_ref.shape[0]
            bx_scr[...] = jnp.zeros_like(bx_scr)
            for m in range(buf_ref.shape[0]):
                for g in range(bx_scr.shape[0]):
                    bx_scr[g, token_rows(first + m), :] = buf_ref[m, :, g * LANES:(g + 1) * LANES]

            def shift(x, s, cols):
                return jnp.where(tok >= s, pltpu.roll(x, s, 0),
                                 pltpu.roll(bx_scr[cols.start // LANES], n - (SAMPLE_LEN - s), 0))
            return shift

        def emit_tail(raw_scr, new_ref):
            first = SAMPLE_LEN - new_ref.shape[0]
            for m in range(new_ref.shape[0]):
                for g in range(raw_scr.shape[0]):
                    new_ref[m, :, g * LANES:(g + 1) * LANES] = raw_scr[g, token_rows(first + m), :]

        shift_a = make_shift(bufa_ref, bxa_scr)
        for gi in range(D_CONV // LANES):
            cols = slice(gi * LANES, (gi + 1) * LANES)
            u, y = _conv_a_group(p_ref, gi, caw_ref, canw_ref, shift_a)
            ua_scr[gi] = u
            y_ref[:, cols] = y
        emit_tail(ua_scr, newa_ref)

        shift_q = make_shift(bufq_ref, bxq_scr)
        qkv = [None] * (3 * N_HEADS)
        for gi in range(3 * D_DN // LANES):
            xq_scr[gi], qkv[gi] = _conv_qkv_group(p_ref, gi, cqw_ref, shift_q)
        emit_tail(xq_scr, newq_ref)

        row, col, incl, strict, eye = _group_masks(n, SAMPLE_LEN)
        inverse = functools.partial(_unit_lower_inverse, eye=eye, base=SAMPLE_LEN, base_mask=None, merge_masks=[])

        gb = gb_ref[...]
        gcum = gb
        s = 1
        while s < SAMPLE_LEN:
            gcum = gcum + jnp.where(tok >= s, pltpu.roll(gcum, s, 0), 0.0)
            s *= 2
        glast = gcum
        for back in range(1, SAMPLE_LEN):
            glast = jnp.where(tok == SAMPLE_LEN - 1 - back, pltpu.roll(gcum, n - back, 0), glast)
        g_scr[...] = gcum
        gcum_t = gcum.T
        e_end = jnp.exp(glast - gcum)
        heads = list(range(N_HEADS))
        gcol = [gcum[:, h:h + 1] for h in heads]
        bcol = [gb[:, N_HEADS + h:N_HEADS + h + 1] for h in heads]
        grow = [gcum_t[h:h + 1, :] for h in heads]
        ub, w, qd, p = _chunk_prepare(qkv[:N_HEADS], qkv[N_HEADS:2 * N_HEADS], qkv[2 * N_HEADS:],
                                            gcol, bcol, grow, incl, strict, inverse)
        for h in heads:
            hc = slice(h * DK, (h + 1) * DK)
            ub_scr[:, hc] = ub[h]
            w_scr[:, hc] = w[h]
            qd_scr[:, hc] = qd[h]
            ke_scr[:, hc] = qkv[N_HEADS + h] * e_end[:, h:h + 1]
            p_scr[h] = p[h]
        u_scr[...] = jnp.zeros_like(u_scr)

    r0 = pl.multiple_of(j * ROWS_PER_STEP, ROWS_PER_STEP)
    row16 = lax.broadcasted_iota(jnp.int32, (2 * SUBLANES, LANES), 0)
    first_of_pair = (row16 % SUBLANES) < SAMPLE_LEN
    coln = lax.broadcasted_iota(jnp.int32, (DK, n), 1)
    heads = list(range(N_HEADS))
    hcs = [slice(h * DK, (h + 1) * DK) for h in heads]
    for h in heads:
        for t in range(ROWS_PER_STEP // SUBLANES):
            rows = pl.ds(pl.multiple_of(r0 + t * SUBLANES, SUBLANES), SUBLANES)
            lhs = jnp.concatenate([w_scr[rows, hcs[h]], qd_scr[rows, hcs[h]]], axis=0).astype(BF16)
            da = _dot_nt(lhs, s_ref[2 * t, h].astype(BF16))
            db = _dot_nt(lhs, s_ref[2 * t + 1, h].astype(BF16))
            ws = jnp.where(first_of_pair, da, db)
            u_scr[rows, hcs[h]] = ub_scr[rows, hcs[h]] - ws[:SUBLANES]
            qs_scr[rows, hcs[h]] = ws[SUBLANES:]
    u_h = [u_scr[:, hc] for hc in hcs]
    u_t = _each(lambda t: t.T.astype(BF16), u_h)
    ke16 = [ke_scr[:, hc].astype(BF16) for hc in hcs]
    for sq in range(SEQ_PER_STEP):
        seq_cols = jnp.where(coln // SAMPLE_LEN == j * SEQ_PER_STEP + sq, 1.0, 0.0).astype(BF16)
        g_end = jnp.exp(g_scr[pl.ds(r0 + sq * SAMPLE_LEN + SAMPLE_LEN - 1, 1), :])
        upd = _each(lambda t, ke: _dot(t * seq_cols, ke), u_t, ke16)
        for h in heads:
            snew_ref[sq, h] = g_end[:, h:h + 1] * s_ref[sq, h] + upd[h]
    rows = pl.ds(r0, ROWS_PER_STEP)
    o = [qs_scr[rows, hcs[h]] + _dot(p_scr[h, rows, :].astype(BF16), u_h[h].astype(BF16)) for h in heads]
    for h in heads:
        y_ref[rows, D_CONV + h * DK:D_CONV + (h + 1) * DK] = _gated_out(
            o[h], p_ref[rows, Z_OFF + h * DK:Z_OFF + (h + 1) * DK], dnw_ref[...])


def _mixer_sample(proj, gb, bufa, bufq, caw, canw, cqw, dnw, state, new_state, *, layer):
    m = proj.shape[0]
    n = CHUNK
    seqs = n // SAMPLE_LEN
    n_seq = m // SAMPLE_LEN
    steps = n // ROWS_PER_STEP
    full = lambda shape: pl.BlockSpec(shape, lambda i, j: (0,) * len(shape))
    rowblk = lambda width: pl.BlockSpec((n, width), lambda i, j: (i, 0))
    sblk = pl.BlockSpec((None, SEQ_PER_STEP, N_HEADS, DK, DK), lambda i, j: (layer, i * steps + j, 0, 0, 0))
    la, lq = bufa.shape[1], bufq.shape[1]
    prev_index = 9
    return pl.pallas_call(
        _mixer_sample_kernel,
        out_shape=(
            jax.ShapeDtypeStruct((m, D_MODEL), BF16),
            jax.ShapeDtypeStruct((la, n_seq, D_CONV), F32),
            jax.ShapeDtypeStruct((lq, n_seq, 3 * D_DN), F32),
            jax.ShapeDtypeStruct(state.shape, F32),
        ),
        grid=(m // n, steps),
        in_specs=[
            rowblk(D_MAIN), rowblk(LANES),
            pl.BlockSpec((None, la, seqs, D_CONV), lambda i, j: (layer, 0, i, 0)),
            pl.BlockSpec((None, lq, seqs, 3 * D_DN), lambda i, j: (layer, 0, i, 0)),
            full((SUBLANES, D_CONV)), full((1, D_CONV)), full((SUBLANES, 3 * D_DN)), full((1, DK)),
            sblk, pl.BlockSpec(memory_space=pl.ANY),
        ],
        input_output_aliases={prev_index: 3} if layer > 0 else {},
        out_specs=(rowblk(D_MODEL),
                   pl.BlockSpec((la, seqs, D_CONV), lambda i, j: (0, i, 0)),
                   pl.BlockSpec((lq, seqs, 3 * D_DN), lambda i, j: (0, i, 0)),
                   sblk),
        scratch_shapes=[
            pltpu.VMEM((n, D_DN), F32),
            pltpu.VMEM((n, D_DN), F32),
            pltpu.VMEM((n, D_DN), F32),
            pltpu.VMEM((n, D_DN), F32),
            pltpu.VMEM((N_HEADS, n, n), F32),
            pltpu.VMEM((n, D_DN), F32),
            pltpu.VMEM((n, D_DN), F32),
            pltpu.VMEM((n, LANES), F32),
            pltpu.VMEM((D_CONV // LANES, n, LANES), F32),
            pltpu.VMEM((D_CONV // LANES, n, LANES), F32),
            pltpu.VMEM((3 * D_DN // LANES, n, LANES), F32),
            pltpu.VMEM((3 * D_DN // LANES, n, LANES), F32),
        ],
        compiler_params=pltpu.CompilerParams(
            dimension_semantics=("arbitrary", "arbitrary"), vmem_limit_bytes=VMEM_LIMIT),
        name="mixer_sample",
    )(proj, gb, bufa, bufq, caw, canw, cqw, dnw, state, new_state)


def _pad_rows(w):
    return jnp.pad(w, ((0, 0), (0, SUBLANES - w.shape[1]), (0, 0)))


def kernel(x_prompt, x_sample, state_conv_a, state_conv_qkv, state_delta, norm_mix_w, w_in,
           conv_a_w, conv_a_norm_w, conv_qkv_w, a_log, dt_bias, dn_norm_w, w_out,
           norm_ffn_w, w_up, w_down, final_norm_w):
    depth = w_in.shape[0]
    n_seq, seq_len, _ = x_prompt.shape
    n_dec, dec_len, _ = x_sample.shape
    assert dec_len == SAMPLE_LEN

    w_in_t = jnp.swapaxes(w_in, 1, 2)
    alog = jnp.pad(a_log, ((0, 0), (0, LANES - N_HEADS)))[:, None, :]
    dtb = jnp.pad(dt_bias, ((0, 0), (0, LANES - N_HEADS)))[:, None, :]
    caw, cqw = _pad_rows(conv_a_w), _pad_rows(conv_qkv_w)
    bufa = jnp.swapaxes(state_conv_a, 1, 2)
    bufq = jnp.swapaxes(state_conv_qkv, 1, 2)

    xp = x_prompt.reshape(n_seq * seq_len, D_MODEL)
    xs = x_sample.reshape(n_dec * dec_len, D_MODEL)
    tm_p, tm_s = 1024, 512
    fw = final_norm_w[None, :]
    conv_a_p, conv_q_p, delta_p, conv_a_s, conv_q_s = [], [], [], [], []
    delta_s = state_delta
    for l in range(depth):
        nmw, nfw = norm_mix_w[l][None, :], norm_ffn_w[l][None, :]
        canw, dnw = conv_a_norm_w[l][None, :], dn_norm_w[l][None, :]
        final = l == depth - 1
        proj, gb, w_main16 = _proj(xs, nmw, w_in_t, w_in_t, alog[l], dtb[l], layer=l, tm=tm_s)
        y, new_a, new_q, delta_s = _mixer_sample(proj, gb, bufa, bufq, caw[l], canw, cqw[l], dnw, state_delta,
                                                 delta_s, layer=l)
        xs, w_out16 = _outproj(y, w_out, xs, layer=l, tm=tm_s)
        xs, w_up16, w_down16 = _ffn(xs, nfw, w_up, w_down, fw, layer=l, tm=tm_s, final=final)
        conv_a_s.append(new_a)
        conv_q_s.append(new_q)
        proj, gb, last_q = _proj_conv(xp, nmw, w_main16, w_in_t, alog[l], dtb[l], cqw[l], layer=l, seq_len=seq_len,
                                      tm=tm_p)
        y, last_a, s_fin = _mixer_prompt(proj, gb, caw[l], canw, dnw, n_seq=n_seq, seq_len=seq_len)
        xp, _ = _outproj(y, w_out16, xp, layer=l, tm=tm_p, tn=1024)
        xp, _, _ = _ffn(xp, nfw, w_up16, w_down16, fw, layer=l, tm=tm_p, final=final)
        conv_a_p.append(last_a[:, SUBLANES - 2:])
        conv_q_p.append(last_q[seq_len // tm_p - 1::seq_len // tm_p, SUBLANES - 3:])
        delta_p.append(s_fin)

    return (xp.reshape(n_seq, seq_len, D_MODEL), xs.reshape(n_dec, dec_len, D_MODEL),
            jnp.stack(conv_a_p), jnp.stack(conv_q_p), jnp.stack(delta_p),
            jnp.swapaxes(jnp.stack(conv_a_s), 1, 2), jnp.swapaxes(jnp.stack(conv_q_s), 1, 2), delta_s)
```

```python
import functools

import jax
import jax.numpy as jnp
from jax import lax
from jax.experimental import pallas as pl
from jax.experimental.pallas import tpu as pltpu

F32 = jnp.float32
BF16 = jnp.bfloat16

D_MODEL = 2048
D_CONV = 1024
D_DN = 1024
N_HEADS = 8
DK = 128
D_FF = 8192
D_MAIN = 3 * D_CONV + 4 * D_DN
QKV_OFF = 3 * D_CONV
Z_OFF = QKV_OFF + 3 * D_DN
EPS = 1e-6
LANES = 128
SUBLANES = 8
CHUNK = 128
SAMPLE_LEN = 4
VMEM_LIMIT = 56 * 1024 * 1024


def _dot(a, b):
    return jnp.dot(a, b, preferred_element_type=F32)


def _dot_nt(a, b):
    return lax.dot_general(a, b, (((1,), (1,)), ((), ())), preferred_element_type=F32)


def _dot_tn(a, b):
    return lax.dot_general(a, b, (((0,), (0,)), ((), ())), preferred_element_type=F32)


def _softplus(x):
    return jnp.maximum(x, 0.0) + jnp.log1p(jnp.exp(-jnp.abs(x)))


def _silu(x):
    return x * jax.nn.sigmoid(x)


def _rms(x):
    return x * lax.rsqrt(jnp.mean(x * x, axis=-1, keepdims=True) + EPS)


def _weight_spec(w, layer, block, index):
    if w.ndim == 2:
        return pl.BlockSpec(block, index, pipeline_mode=pl.Buffered(1) if block == w.shape else None)
    return pl.BlockSpec((None,) + block, lambda *g: (layer,) + index(*g))


def _bf16_weight_ref(w_ref, copy_ref):
    if copy_ref is None:
        return w_ref
    copy_ref[...] = w_ref[...].astype(BF16)
    return copy_ref


def _proj_kernel(x_ref, nw_ref, w_ref, wab_ref, alog_ref, dtb_ref, proj_ref, gb_ref, *rest, tm, emit):
    w16_ref, h_scr, wab_scr = rest if emit else (None,) + rest
    w = _bf16_weight_ref(w_ref, w16_ref)
    first = pl.program_id(1) == 0

    @pl.when(first)
    def _():
        wab_scr[...] = jnp.zeros_like(wab_scr)
        wab_scr[:2 * N_HEADS, :] = wab_ref[...].astype(BF16)
        for r in range(0, tm, 256):
            rows = slice(r, r + 256)
            h = (_rms(x_ref[rows, :]) * nw_ref[...]).astype(BF16)
            h_scr[rows, :] = h
            ab = _dot_nt(h, wab_scr[...])
            lane = lax.broadcasted_iota(jnp.int32, ab.shape, 1)
            g = -jnp.exp(alog_ref[...]) * _softplus(ab + dtb_ref[...])
            gb_ref[rows, :] = jnp.where(lane < N_HEADS, g, jax.nn.sigmoid(ab))
            proj_ref[rows, :] = _dot_nt(h, w[...])

    @pl.when(jnp.logical_not(first))
    def _():
        proj_ref[...] = _dot_nt(h_scr[...], w[...])


def _emits_bf16(w, m, tm):
    emit = w.dtype == F32
    assert not emit or m == tm
    return emit


def _proj(x, nw, w, w_in_t, alog, dtb, *, layer, tm, tn=512):
    m = x.shape[0]
    emit = _emits_bf16(w, m, tm)
    n_gate = 2 * N_HEADS
    out_shape = [jax.ShapeDtypeStruct((m, D_MAIN), F32), jax.ShapeDtypeStruct((m, LANES), F32)]
    out_specs = [pl.BlockSpec((tm, tn), lambda i, j: (i, j)), pl.BlockSpec((tm, LANES), lambda i, j: (i, 0))]
    if emit:
        out_shape.append(jax.ShapeDtypeStruct((D_MAIN, D_MODEL), BF16))
        out_specs.append(pl.BlockSpec((tn, D_MODEL), lambda i, j: (j, 0)))
    outs = pl.pallas_call(
        functools.partial(_proj_kernel, tm=tm, emit=emit),
        out_shape=tuple(out_shape),
        grid=(m // tm, D_MAIN // tn),
        in_specs=[
            pl.BlockSpec((tm, D_MODEL), lambda i, j: (i, 0)),
            pl.BlockSpec((1, D_MODEL), lambda i, j: (0, 0)),
            _weight_spec(w, layer, (tn, D_MODEL), lambda i, j: (j, 0)),
            pl.BlockSpec((None, n_gate, D_MODEL), lambda i, j: (layer, D_MAIN // n_gate, 0)),
            pl.BlockSpec((1, LANES), lambda i, j: (0, 0)),
            pl.BlockSpec((1, LANES), lambda i, j: (0, 0)),
        ],
        out_specs=tuple(out_specs),
        scratch_shapes=[pltpu.VMEM((tm, D_MODEL), BF16), pltpu.VMEM((LANES, D_MODEL), BF16)],
        compiler_params=pltpu.CompilerParams(
            dimension_semantics=("arbitrary", "arbitrary"), vmem_limit_bytes=VMEM_LIMIT),
        name="norm_proj",
    )(x, nw, w, w_in_t, alog, dtb)
    return outs if emit else (*outs, w)


def _outproj_kernel(y_ref, w_ref, x_ref, o_ref, w16_ref=None):
    o_ref[...] = x_ref[...] + _dot(y_ref[...], _bf16_weight_ref(w_ref, w16_ref)[...])


def _outproj(y, w, x, *, layer, tm, tn=512):
    m = x.shape[0]
    emit = _emits_bf16(w, m, tm)
    out_shape = [jax.ShapeDtypeStruct((m, D_MODEL), F32)]
    out_specs = [pl.BlockSpec((tm, tn), lambda i, j: (i, j))]
    if emit:
        out_shape.append(jax.ShapeDtypeStruct((D_MODEL, D_MODEL), BF16))
        out_specs.append(pl.BlockSpec((D_MODEL, tn), lambda i, j: (0, j)))
    outs = pl.pallas_call(
        _outproj_kernel,
        out_shape=tuple(out_shape),
        grid=(m // tm, D_MODEL // tn),
        in_specs=[
            pl.BlockSpec((tm, D_MODEL), lambda i, j: (i, 0)),
            _weight_spec(w, layer, (D_MODEL, tn), lambda i, j: (0, j)),
            pl.BlockSpec((tm, tn), lambda i, j: (i, j)),
        ],
        out_specs=tuple(out_specs),
        compiler_params=pltpu.CompilerParams(
            dimension_semantics=("arbitrary", "arbitrary"), vmem_limit_bytes=VMEM_LIMIT),
        name="out_proj",
    )(y, w, x)
    return outs if emit else (*outs, w)


def _ffn_kernel(x_ref, nw_ref, wu_ref, wd_ref, fw_ref, o_ref, *rest, tm, final, emit):
    wu16_ref, wd16_ref, h_scr = rest if emit else (None, None) + rest
    f = pl.program_id(1)
    wu = _bf16_weight_ref(wu_ref, wu16_ref)
    wd = _bf16_weight_ref(wd_ref, wd16_ref)

    def add_mlp(h, rows, start):
        for c in range(0, wu.shape[1], 512):
            up = jnp.maximum(_dot(h, wu[:, c:c + 512]), 0.0)
            part = _dot((up * up).astype(BF16), wd[c:c + 512, :])
            if c == 0 and start is not None:
                o_ref[rows, :] = start + part
            else:
                o_ref[rows, :] += part

    @pl.when(f == 0)
    def _():
        for r in range(0, tm, 256):
            rows = slice(r, r + 256)
            x = x_ref[rows, :]
            h = (_rms(x) * nw_ref[...]).astype(BF16)
            h_scr[rows, :] = h
            add_mlp(h, rows, x)

    @pl.when(f > 0)
    def _():
        add_mlp(h_scr[...], slice(None), None)

    if final:
        @pl.when(f == pl.num_programs(1) - 1)
        def _():
            for r in range(0, tm, 256):
                rows = slice(r, r + 256)
                o_ref[rows, :] = _rms(o_ref[rows, :]) * fw_ref[...]


def _ffn(x, nw, wu, wd, fw, *, layer, tm, final, tf=512):
    m = x.shape[0]
    emit = _emits_bf16(wu, m, tm)
    assert (wd.dtype == F32) == emit
    out_shape = [jax.ShapeDtypeStruct((m, D_MODEL), F32)]
    out_specs = [pl.BlockSpec((tm, D_MODEL), lambda i, f: (i, 0))]
    if emit:
        out_shape += [jax.ShapeDtypeStruct((D_MODEL, D_FF), BF16), jax.ShapeDtypeStruct((D_FF, D_MODEL), BF16)]
        out_specs += [pl.BlockSpec((D_MODEL, tf), lambda i, f: (0, f)), pl.BlockSpec((tf, D_MODEL), lambda i, f: (f, 0))]
    outs = pl.pallas_call(
        functools.partial(_ffn_kernel, tm=tm, final=final, emit=emit),
        out_shape=tuple(out_shape),
        grid=(m // tm, D_FF // tf),
        in_specs=[
            pl.BlockSpec((tm, D_MODEL), lambda i, f: (i, 0)),
            pl.BlockSpec((1, D_MODEL), lambda i, f: (0, 0)),
            _weight_spec(wu, layer, (D_MODEL, tf), lambda i, f: (0, f)),
            _weight_spec(wd, layer, (tf, D_MODEL), lambda i, f: (f, 0)),
            pl.BlockSpec((1, D_MODEL), lambda i, f: (0, 0)),
        ],
        out_specs=tuple(out_specs),
        scratch_shapes=[pltpu.VMEM((tm, D_MODEL), BF16)],
        compiler_params=pltpu.CompilerParams(
            dimension_semantics=("arbitrary", "arbitrary"), vmem_limit_bytes=VMEM_LIMIT),
        name="ffn",
    )(x, nw, wu, wd, fw)
    return outs if emit else (*outs, wu, wd)


def _group_masks(n, group):
    row = lax.broadcasted_iota(jnp.int32, (n, n), 0)
    col = lax.broadcasted_iota(jnp.int32, (n, n), 1)
    same = (row // group) == (col // group)
    incl = same & (row >= col)
    strict = same & (row > col)
    eye = jnp.where(row == col, 1.0, 0.0).astype(F32)
    return row, col, incl, strict, eye


def _merge_masks(row, col, base, group):
    masks = []
    b = base
    while b < group:
        masks.append(((row // (2 * b)) == (col // (2 * b))) & ((row // b) != (col // b)))
        b *= 2
    return masks


def _each(f, *lists):
    return [f(*args) for args in zip(*lists)]


def _unit_lower_inverse(a, eye, base, base_mask, merge_masks):
    n1 = a if base_mask is None else _each(lambda t: jnp.where(base_mask, t, 0.0), a)
    n1b = _each(lambda t: t.astype(BF16), n1)
    n2 = _each(_dot, n1b, n1b)
    n2b = _each(lambda t: t.astype(BF16), n2)
    n3 = _each(_dot, n1b, n2b)
    x = _each(lambda p1, p2, p3: eye - p1 + p2 - p3, n1, n2, n3)
    if base == 8:
        n4b = _each(lambda t: _dot(t, t).astype(BF16), n2b)
        x = _each(lambda t, p4: t + _dot(t.astype(BF16), p4), x, n4b)
    for m in merge_masks:
        xb = _each(lambda t: t.astype(BF16), x)
        xl = _each(lambda tb, t: _dot(tb, jnp.where(m, t, 0.0).astype(BF16)).astype(BF16), xb, a)
        x = _each(lambda t, l, tb: t - _dot(l, tb), x, xl, xb)
    return x


def _chunk_prepare(q, k, v, gcol, bcol, grow, incl, strict, inverse):
    decay = _each(lambda gc, gr: jnp.exp(jnp.where(incl, gc - gr, -jnp.inf)), gcol, grow)
    kb = _each(lambda t: t.astype(BF16), k)
    kk = _each(_dot_nt, kb, kb)
    qk = _each(lambda t, tb: _dot_nt(t.astype(BF16), tb), q, kb)
    a = _each(lambda b, d, m: jnp.where(strict, b * d * m, 0.0), bcol, decay, kk)
    t = inverse(a)
    eg = _each(jnp.exp, gcol)
    rhs = _each(lambda b, e, vv, kx: jnp.concatenate([b * vv, (b * e) * kx], axis=1).astype(BF16), bcol, eg, v, k)
    sol = _each(lambda tt, r: _dot(tt.astype(BF16), r), t, rhs)
    return (_each(lambda s: s[:, :DK], sol), _each(lambda s: s[:, DK:], sol),
            _each(lambda qq, e: qq * e, q, eg), _each(lambda m, d: m * d, qk, decay))


def _gated_out(o, z, dnw):
    return (_rms(o) * dnw * _silu(z)).astype(BF16)


def _conv_a_group(p_ref, gi, caw_ref, canw_ref, shift):
    cols = slice(gi * LANES, (gi + 1) * LANES)
    c_cols = slice(D_CONV + gi * LANES, D_CONV + (gi + 1) * LANES)
    h_cols = slice(2 * D_CONV + gi * LANES, 2 * D_CONV + (gi + 1) * LANES)
    b_a = p_ref[:, cols]
    u = p_ref[:, c_cols] * p_ref[:, h_cols]
    conv = (caw_ref[2:3, cols] * u + caw_ref[1:2, cols] * shift(u, 1, cols)
            + caw_ref[0:1, cols] * shift(u, 2, cols))
    y = b_a * conv
    return u, (_rms(y) * canw_ref[:, cols]).astype(BF16)


def _conv_qkv_group(p_ref, gi, cqw_ref, shift):
    cols = slice(gi * LANES, (gi + 1) * LANES)
    x_cols = slice(QKV_OFF + gi * LANES, QKV_OFF + (gi + 1) * LANES)
    x = p_ref[:, x_cols]
    conv = (cqw_ref[3:4, cols] * x + cqw_ref[2:3, cols] * shift(x, 1, cols)
            + cqw_ref[1:2, cols] * shift(x, 2, cols) + cqw_ref[0:1, cols] * shift(x, 3, cols))
    c = _silu(conv)
    if gi < 2 * N_HEADS:
        c = c * lax.rsqrt(jnp.sum(c * c, axis=-1, keepdims=True) + EPS)
        if gi < N_HEADS:
            c = c * (DK ** -0.5)
    return x, c


def _mixer_prompt_kernel(p_ref, gb_ref, caw_ref, canw_ref, cqw_ref, dnw_ref,
                         y_ref, lasta_ref, lastq_ref, sfin_ref,
                         qkv_scr, ua_carry, xq_carry, st_scr, *, rb):
    nb = pl.program_id(1)

    @pl.when(nb == 0)
    def _():
        ua_carry[...] = jnp.zeros_like(ua_carry)
        xq_carry[...] = jnp.zeros_like(xq_carry)
        st_scr[...] = jnp.zeros_like(st_scr)

    row8 = lax.broadcasted_iota(jnp.int32, (SUBLANES, LANES), 0)

    def make_shift(carry_ref):
        def shift(x, s, cols):
            rolled = pltpu.roll(x, s, 0)
            prev = pltpu.roll(carry_ref[:, cols], s, 0)
            first = jnp.where(row8 < s, prev, rolled[:SUBLANES])
            return jnp.concatenate([first, rolled[SUBLANES:]], axis=0)
        return shift

    shift_a = make_shift(ua_carry)
    for gi in range(D_CONV // LANES):
        cols = slice(gi * LANES, (gi + 1) * LANES)
        u, y = _conv_a_group(p_ref, gi, caw_ref, canw_ref, shift_a)
        ua_carry[:, cols] = u[rb - SUBLANES:]
        y_ref[:, cols] = y

    shift_q = make_shift(xq_carry)
    for gi in range(3 * D_DN // LANES):
        cols = slice(gi * LANES, (gi + 1) * LANES)
        x, c = _conv_qkv_group(p_ref, gi, cqw_ref, shift_q)
        xq_carry[:, cols] = x[rb - SUBLANES:]
        qkv_scr[:, cols] = c

    row, col, incl, strict, eye = _group_masks(CHUNK, CHUNK)
    base_mask = (row // 8) == (col // 8)
    merges = _merge_masks(row, col, 8, CHUNK)
    inverse = functools.partial(_unit_lower_inverse, eye=eye, base=8, base_mask=base_mask, merge_masks=merges)
    rowc = lax.broadcasted_iota(jnp.int32, (CHUNK, LANES), 0)

    heads = list(range(N_HEADS))
    chunks = list(range(rb // CHUNK))
    q, k, v, gcol, bcol, grow = [], [], [], [], [], []
    for c in chunks:
        rows = slice(c * CHUNK, (c + 1) * CHUNK)
        gb = gb_ref[rows, :]
        gcum = gb
        s = 1
        while s < CHUNK:
            gcum = gcum + jnp.where(rowc >= s, pltpu.roll(gcum, s, 0), 0.0)
            s *= 2
        gcum_t = gcum.T
        q += [qkv_scr[rows, h * DK:(h + 1) * DK] for h in heads]
        k += [qkv_scr[rows, D_DN + h * DK:D_DN + (h + 1) * DK] for h in heads]
        v += [qkv_scr[rows, 2 * D_DN + h * DK:2 * D_DN + (h + 1) * DK] for h in heads]
        gcol += [gcum[:, h:h + 1] for h in heads]
        bcol += [gb[:, N_HEADS + h:N_HEADS + h + 1] for h in heads]
        grow += [gcum_t[h:h + 1, :] for h in heads]
    ub, w, qd, p = _chunk_prepare(q, k, v, gcol, bcol, grow, incl, strict, inverse)
    ke = _each(lambda a, g: (a * jnp.exp(g[CHUNK - 1:CHUNK, :] - g)).astype(BF16), k, gcol)
    lhs = _each(lambda a, b: jnp.concatenate([a, b], axis=0).astype(BF16), w, qd)
    p16 = _each(lambda a: a.astype(BF16), p)
    g_end = _each(lambda g: jnp.exp(g[CHUNK - 1:CHUNK, :]), gcol)

    st = [st_scr[h] for h in heads]
    for c in chunks:
        rows = slice(c * CHUNK, (c + 1) * CHUNK)
        sl = slice(c * N_HEADS, (c + 1) * N_HEADS)
        ws = _each(lambda a, s: _dot(a, s.astype(BF16)), lhs[sl], st)
        u16 = _each(lambda a, b: (a - b[:CHUNK]).astype(BF16), ub[sl], ws)
        o = _each(lambda a, b, c: a[CHUNK:] + _dot(b, c), ws, p16[sl], u16)
        st = _each(lambda g, s, a, b: g * s + _dot_tn(a, b), g_end[sl], st, ke[sl], u16)
        for h in heads:
            y_ref[rows, D_CONV + h * DK:D_CONV + (h + 1) * DK] = _gated_out(
                o[h], p_ref[rows, Z_OFF + h * DK:Z_OFF + (h + 1) * DK], dnw_ref[...])
    for h in heads:
        st_scr[h] = st[h]

    @pl.when(nb == pl.num_programs(1) - 1)
    def _():
        lasta_ref[0] = ua_carry[...]
        lastq_ref[0] = xq_carry[...]
        for h in range(N_HEADS):
            sfin_ref[0, h] = st_scr[h].T


def _mixer_prompt(proj, gb, caw, canw, cqw, dnw, *, n_seq, seq_len, rb=256):
    nblk = seq_len // rb
    m = n_seq * seq_len
    full = lambda shape: pl.BlockSpec(shape, lambda b, n: (0,) * len(shape))
    return pl.pallas_call(
        functools.partial(_mixer_prompt_kernel, rb=rb),
        out_shape=(
            jax.ShapeDtypeStruct((m, D_MODEL), BF16),
            jax.ShapeDtypeStruct((n_seq, SUBLANES, D_CONV), F32),
            jax.ShapeDtypeStruct((n_seq, SUBLANES, 3 * D_DN), F32),
            jax.ShapeDtypeStruct((n_seq, N_HEADS, DK, DK), F32),
        ),
        grid=(n_seq, nblk),
        in_specs=[
            pl.BlockSpec((rb, D_MAIN), lambda b, n: (b * nblk + n, 0)),
            pl.BlockSpec((rb, LANES), lambda b, n: (b * nblk + n, 0)),
            full((SUBLANES, D_CONV)),
            full((1, D_CONV)),
            full((SUBLANES, 3 * D_DN)),
            full((1, DK)),
        ],
        out_specs=(
            pl.BlockSpec((rb, D_MODEL), lambda b, n: (b * nblk + n, 0)),
            pl.BlockSpec((1, SUBLANES, D_CONV), lambda b, n: (b, 0, 0)),
            pl.BlockSpec((1, SUBLANES, 3 * D_DN), lambda b, n: (b, 0, 0)),
            pl.BlockSpec((1, N_HEADS, DK, DK), lambda b, n: (b, 0, 0, 0)),
        ),
        scratch_shapes=[
            pltpu.VMEM((rb, 3 * D_DN), F32),
            pltpu.VMEM((SUBLANES, D_CONV), F32),
            pltpu.VMEM((SUBLANES, 3 * D_DN), F32),
            pltpu.VMEM((N_HEADS, DK, DK), F32),
        ],
        compiler_params=pltpu.CompilerParams(
            dimension_semantics=("arbitrary", "arbitrary"), vmem_limit_bytes=VMEM_LIMIT),
        name="mixer_prompt",
    )(proj, gb, caw, canw, cqw, dnw)


SEQ_PER_STEP = 8
ROWS_PER_STEP = SEQ_PER_STEP * SAMPLE_LEN


def _mixer_sample_kernel(p_ref, gb_ref, bufa_ref, bufq_ref, caw_ref, canw_ref, cqw_ref, dnw_ref, s_ref, prev_ref,
                         y_ref, newa_ref, newq_ref, snew_ref,
                         w_scr, qd_scr, ub_scr, ke_scr, p_scr, u_scr, qs_scr, g_scr, ua_scr, bxa_scr, xq_scr, bxq_scr):
    del prev_ref
    j = pl.program_id(1)
    n = CHUNK
    seqs = n // SAMPLE_LEN

    def token_rows(t):
        return pl.ds(t, seqs, stride=SAMPLE_LEN)

    @pl.when(j == 0)
    def _():
        rown = lax.broadcasted_iota(jnp.int32, (n, LANES), 0)
        tok = rown % SAMPLE_LEN

        def make_shift(buf_ref, bx_scr):
            first = SAMPLE_LEN - buf_ref.shape[0]
            bx_scr[...] = jnp.zeros_like(bx_scr)
            for m in range(buf_ref.shape[0]):
                for g in range(bx_scr.shape[0]):
                    bx_scr[g, token_rows(first + m), :] = buf_ref[m, :, g * LANES:(g + 1) * LANES]

            def shift(x, s, cols):
                return jnp.where(tok >= s, pltpu.roll(x, s, 0),
                                 pltpu.roll(bx_scr[cols.start // LANES], n - (SAMPLE_LEN - s), 0))
            return shift

        def emit_tail(raw_scr, new_ref):
            first = SAMPLE_LEN - new_ref.shape[0]
            for m in range(new_ref.shape[0]):
                for g in range(raw_scr.shape[0]):
                    new_ref[m, :, g * LANES:(g + 1) * LANES] = raw_scr[g, token_rows(first + m), :]

        shift_a = make_shift(bufa_ref, bxa_scr)
        for gi in range(D_CONV // LANES):
            cols = slice(gi * LANES, (gi + 1) * LANES)
            u, y = _conv_a_group(p_ref, gi, caw_ref, canw_ref, shift_a)
            ua_scr[gi] = u
            y_ref[:, cols] = y
        emit_tail(ua_scr, newa_ref)

        shift_q = make_shift(bufq_ref, bxq_scr)
        qkv = [None] * (3 * N_HEADS)
        for gi in range(3 * D_DN // LANES):
            xq_scr[gi], qkv[gi] = _conv_qkv_group(p_ref, gi, cqw_ref, shift_q)
        emit_tail(xq_scr, newq_ref)

        row, col, incl, strict, eye = _group_masks(n, SAMPLE_LEN)
        inverse = functools.partial(_unit_lower_inverse, eye=eye, base=SAMPLE_LEN, base_mask=None, merge_masks=[])

        gb = gb_ref[...]
        gcum = gb
        s = 1
        while s < SAMPLE_LEN:
            gcum = gcum + jnp.where(tok >= s, pltpu.roll(gcum, s, 0), 0.0)
            s *= 2
        glast = gcum
        for back in range(1, SAMPLE_LEN):
            glast = jnp.where(tok == SAMPLE_LEN - 1 - back, pltpu.roll(gcum, n - back, 0), glast)
        g_scr[...] = gcum
        gcum_t = gcum.T
        e_end = jnp.exp(glast - gcum)
        heads = list(range(N_HEADS))
        gcol = [gcum[:, h:h + 1] for h in heads]
        bcol = [gb[:, N_HEADS + h:N_HEADS + h + 1] for h in heads]
        grow = [gcum_t[h:h + 1, :] for h in heads]
        ub, w, qd, p = _chunk_prepare(qkv[:N_HEADS], qkv[N_HEADS:2 * N_HEADS], qkv[2 * N_HEADS:],
                                      gcol, bcol, grow, incl, strict, inverse)
        for h in heads:
            hc = slice(h * DK, (h + 1) * DK)
            ub_scr[:, hc] = ub[h]
            w_scr[:, hc] = w[h]
            qd_scr[:, hc] = qd[h]
            ke_scr[:, hc] = qkv[N_HEADS + h] * e_end[:, h:h + 1]
            p_scr[h] = p[h]
        u_scr[...] = jnp.zeros_like(u_scr)

    r0 = pl.multiple_of(j * ROWS_PER_STEP, ROWS_PER_STEP)
    row16 = lax.broadcasted_iota(jnp.int32, (2 * SUBLANES, LANES), 0)
    first_of_pair = (row16 % SUBLANES) < SAMPLE_LEN
    coln = lax.broadcasted_iota(jnp.int32, (DK, n), 1)
    heads = list(range(N_HEADS))
    hcs = [slice(h * DK, (h + 1) * DK) for h in heads]
    for h in heads:
        for t in range(ROWS_PER_STEP // SUBLANES):
            rows = pl.ds(pl.multiple_of(r0 + t * SUBLANES, SUBLANES), SUBLANES)
            lhs = jnp.concatenate([w_scr[rows, hcs[h]], qd_scr[rows, hcs[h]]], axis=0).astype(BF16)
            da = _dot_nt(lhs, s_ref[2 * t, h].astype(BF16))
            db = _dot_nt(lhs, s_ref[2 * t + 1, h].astype(BF16))
            ws = jnp.where(first_of_pair, da, db)
            u_scr[rows, hcs[h]] = ub_scr[rows, hcs[h]] - ws[:SUBLANES]
            qs_scr[rows, hcs[h]] = ws[SUBLANES:]
    u_h = [u_scr[:, hc] for hc in hcs]
    u_t = _each(lambda t: t.T.astype(BF16), u_h)
    ke16 = [ke_scr[:, hc].astype(BF16) for hc in hcs]
    for sq in range(SEQ_PER_STEP):
        seq_cols = jnp.where(coln // SAMPLE_LEN == j * SEQ_PER_STEP + sq, 1.0, 0.0).astype(BF16)
        g_end = jnp.exp(g_scr[pl.ds(r0 + sq * SAMPLE_LEN + SAMPLE_LEN - 1, 1), :])
        upd = _each(lambda t, ke: _dot(t * seq_cols, ke), u_t, ke16)
        for h in heads:
            snew_ref[sq, h] = g_end[:, h:h + 1] * s_ref[sq, h] + upd[h]
    rows = pl.ds(r0, ROWS_PER_STEP)
    o = [qs_scr[rows, hcs[h]] + _dot(p_scr[h, rows, :].astype(BF16), u_h[h].astype(BF16)) for h in heads]
    for h in heads:
        y_ref[rows, D_CONV + h * DK:D_CONV + (h + 1) * DK] = _gated_out(
            o[h], p_ref[rows, Z_OFF + h * DK:Z_OFF + (h + 1) * DK], dnw_ref[...])


def _mixer_sample(proj, gb, bufa, bufq, caw, canw, cqw, dnw, state, new_state, *, layer):
    m = proj.shape[0]
    n = CHUNK
    seqs = n // SAMPLE_LEN
    n_seq = m // SAMPLE_LEN
    steps = n // ROWS_PER_STEP
    full = lambda shape: pl.BlockSpec(shape, lambda i, j: (0,) * len(shape))
    rowblk = lambda width: pl.BlockSpec((n, width), lambda i, j: (i, 0))
    sblk = pl.BlockSpec((None, SEQ_PER_STEP, N_HEADS, DK, DK), lambda i, j: (layer, i * steps + j, 0, 0, 0))
    la, lq = bufa.shape[1], bufq.shape[1]
    prev_index = 9
    return pl.pallas_call(
        _mixer_sample_kernel,
        out_shape=(
            jax.ShapeDtypeStruct((m, D_MODEL), BF16),
            jax.ShapeDtypeStruct((la, n_seq, D_CONV), F32),
            jax.ShapeDtypeStruct((lq, n_seq, 3 * D_DN), F32),
            jax.ShapeDtypeStruct(state.shape, F32),
        ),
        grid=(m // n, steps),
        in_specs=[
            rowblk(D_MAIN), rowblk(LANES),
            pl.BlockSpec((None, la, seqs, D_CONV), lambda i, j: (layer, 0, i, 0)),
            pl.BlockSpec((None, lq, seqs, 3 * D_DN), lambda i, j: (layer, 0, i, 0)),
            full((SUBLANES, D_CONV)), full((1, D_CONV)), full((SUBLANES, 3 * D_DN)), full((1, DK)),
            sblk, pl.BlockSpec(memory_space=pl.ANY),
        ],
        input_output_aliases={prev_index: 3} if layer > 0 else {},
        out_specs=(rowblk(D_MODEL),
                   pl.BlockSpec((la, seqs, D_CONV), lambda i, j: (0, i, 0)),
                   pl.BlockSpec((lq, seqs, 3 * D_DN), lambda i, j: (0, i, 0)),
                   sblk),
        scratch_shapes=[
            pltpu.VMEM((n, D_DN), F32),
            pltpu.VMEM((n, D_DN), F32),
            pltpu.VMEM((n, D_DN), F32),
            pltpu.VMEM((n, D_DN), F32),
            pltpu.VMEM((N_HEADS, n, n), F32),
            pltpu.VMEM((n, D_DN), F32),
            pltpu.VMEM((n, D_DN), F32),
            pltpu.VMEM((n, LANES), F32),
            pltpu.VMEM((D_CONV // LANES, n, LANES), F32),
            pltpu.VMEM((D_CONV // LANES, n, LANES), F32),
            pltpu.VMEM((3 * D_DN // LANES, n, LANES), F32),
            pltpu.VMEM((3 * D_DN // LANES, n, LANES), F32),
        ],
        compiler_params=pltpu.CompilerParams(
            dimension_semantics=("arbitrary", "arbitrary"), vmem_limit_bytes=VMEM_LIMIT),
        name="mixer_sample",
    )(proj, gb, bufa, bufq, caw, canw, cqw, dnw, state, new_state)


def _pad_rows(w):
    return jnp.pad(w, ((0, 0), (0, SUBLANES - w.shape[1]), (0, 0)))


def kernel(x_prompt, x_sample, state_conv_a, state_conv_qkv, state_delta, norm_mix_w, w_in,
           conv_a_w, conv_a_norm_w, conv_qkv_w, a_log, dt_bias, dn_norm_w, w_out,
           norm_ffn_w, w_up, w_down, final_norm_w):
    depth = w_in.shape[0]
    n_seq, seq_len, _ = x_prompt.shape
    n_dec, dec_len, _ = x_sample.shape
    assert dec_len == SAMPLE_LEN

    w_in_t = jnp.swapaxes(w_in, 1, 2)
    alog = jnp.pad(a_log, ((0, 0), (0, LANES - N_HEADS)))[:, None, :]
    dtb = jnp.pad(dt_bias, ((0, 0), (0, LANES - N_HEADS)))[:, None, :]
    caw, cqw = _pad_rows(conv_a_w), _pad_rows(conv_qkv_w)
    bufa = jnp.swapaxes(state_conv_a, 1, 2)
    bufq = jnp.swapaxes(state_conv_qkv, 1, 2)

    xp = x_prompt.reshape(n_seq * seq_len, D_MODEL)
    xs = x_sample.reshape(n_dec * dec_len, D_MODEL)
    tm_p, tm_s = 1024, 512
    fw = final_norm_w[None, :]
    conv_a_p, conv_q_p, delta_p, conv_a_s, conv_q_s = [], [], [], [], []
    delta_s = state_delta
    for l in range(depth):
        nmw, nfw = norm_mix_w[l][None, :], norm_ffn_w[l][None, :]
        canw, dnw = conv_a_norm_w[l][None, :], dn_norm_w[l][None, :]
        final = l == depth - 1
        proj, gb, w_main16 = _proj(xs, nmw, w_in_t, w_in_t, alog[l], dtb[l], layer=l, tm=tm_s)
        y, new_a, new_q, delta_s = _mixer_sample(proj, gb, bufa, bufq, caw[l], canw, cqw[l], dnw, state_delta,
                                                 delta_s, layer=l)
        xs, w_out16 = _outproj(y, w_out, xs, layer=l, tm=tm_s)
        xs, w_up16, w_down16 = _ffn(xs, nfw, w_up, w_down, fw, layer=l, tm=tm_s, final=final)
        conv_a_s.append(new_a)
        conv_q_s.append(new_q)
        proj, gb, _ = _proj(xp, nmw, w_main16, w_in_t, alog[l], dtb[l], layer=l, tm=tm_p, tn=1792)
        y, last_a, last_q, s_fin = _mixer_prompt(proj, gb, caw[l], canw, cqw[l], dnw, n_seq=n_seq, seq_len=seq_len)
        xp, _ = _outproj(y, w_out16, xp, layer=l, tm=512, tn=D_MODEL)
        xp, _, _ = _ffn(xp, nfw, w_up16, w_down16, fw, layer=l, tm=tm_p, final=final)
        conv_a_p.append(last_a[:, SUBLANES - 2:])
        conv_q_p.append(last_q[:, SUBLANES - 3:])
        delta_p.append(s_fin)

    return (xp.reshape(n_seq, seq_len, D_MODEL), xs.reshape(n_dec, dec_len, D_MODEL),
            jnp.stack(conv_a_p), jnp.stack(conv_q_p), jnp.stack(delta_p),
            jnp.swapaxes(jnp.stack(conv_a_s), 1, 2), jnp.swapaxes(jnp.stack(conv_q_s), 1, 2), delta_s)
```

```python
import functools

import jax
import jax.numpy as jnp
from jax import lax
from jax.experimental import pallas as pl
from jax.experimental.pallas import tpu as pltpu

F32 = jnp.float32
BF16 = jnp.bfloat16

D_MODEL = 2048
D_CONV = 1024
D_DN = 1024
N_HEADS = 8
DK = 128
D_FF = 8192
D_MAIN = 3 * D_CONV + 4 * D_DN
QKV_OFF = 3 * D_CONV
Z_OFF = QKV_OFF + 3 * D_DN
EPS = 1e-6
LANES = 128
SUBLANES = 8
CHUNK = 128
SAMPLE_LEN = 4
VMEM_LIMIT = 56 * 1024 * 1024

PROMPT_TILES = dict(proj=(1024, 1792), out=(512, D_MODEL), ffn=(1024, 512))
SAMPLE_TILES = dict(proj=(512, 1024), out=(512, 1024), ffn=(512, 512))
ROW_SUB = 256
HIDDEN_SUB = 512
MIXER_ROWS = 256


def _dot(a, b):
    return jnp.dot(a, b, preferred_element_type=F32)


def _dot_nt(a, b):
    return lax.dot_general(a, b, (((1,), (1,)), ((), ())), preferred_element_type=F32)


def _dot_tn(a, b):
    return lax.dot_general(a, b, (((0,), (0,)), ((), ())), preferred_element_type=F32)


def _softplus(x):
    return jnp.maximum(x, 0.0) + jnp.log1p(jnp.exp(-jnp.abs(x)))


def _silu(x):
    return x * jax.nn.sigmoid(x)


def _rms(x):
    return x * lax.rsqrt(jnp.mean(x * x, axis=-1, keepdims=True) + EPS)


def _weight_spec(w, layer, block, index):
    if w.ndim == 2:
        return pl.BlockSpec(block, index, pipeline_mode=pl.Buffered(1) if block == w.shape else None)
    return pl.BlockSpec((None,) + block, lambda *g: (layer,) + index(*g))


def _bf16_weight_ref(w_ref, copy_ref):
    if copy_ref is None:
        return w_ref
    copy_ref[...] = w_ref[...].astype(BF16)
    return copy_ref


def _proj_kernel(x_ref, nw_ref, w_ref, wab_ref, alog_ref, dtb_ref, proj_ref, gb_ref, *rest, tm, emit):
    w16_ref, h_scr, wab_scr = rest if emit else (None,) + rest
    w = _bf16_weight_ref(w_ref, w16_ref)
    first = pl.program_id(1) == 0

    @pl.when(first)
    def _():
        wab_scr[...] = jnp.zeros_like(wab_scr)
        wab_scr[:2 * N_HEADS, :] = wab_ref[...].astype(BF16)
        for r in range(0, tm, ROW_SUB):
            rows = slice(r, r + ROW_SUB)
            h = (_rms(x_ref[rows, :]) * nw_ref[...]).astype(BF16)
            h_scr[rows, :] = h
            ab = _dot_nt(h, wab_scr[...])
            lane = lax.broadcasted_iota(jnp.int32, ab.shape, 1)
            g = -jnp.exp(alog_ref[...]) * _softplus(ab + dtb_ref[...])
            gb_ref[rows, :] = jnp.where(lane < N_HEADS, g, jax.nn.sigmoid(ab))
            proj_ref[rows, :] = _dot_nt(h, w[...])

    @pl.when(jnp.logical_not(first))
    def _():
        proj_ref[...] = _dot_nt(h_scr[...], w[...])


def _emits_bf16(w, m, tm):
    emit = w.dtype == F32
    assert not emit or m == tm
    return emit


def _proj(x, nw, w, w_in_t, alog, dtb, *, layer, tm, tn):
    m = x.shape[0]
    emit = _emits_bf16(w, m, tm)
    n_gate = 2 * N_HEADS
    out_shape = [jax.ShapeDtypeStruct((m, D_MAIN), F32), jax.ShapeDtypeStruct((m, LANES), F32)]
    out_specs = [pl.BlockSpec((tm, tn), lambda i, j: (i, j)), pl.BlockSpec((tm, LANES), lambda i, j: (i, 0))]
    if emit:
        out_shape.append(jax.ShapeDtypeStruct((D_MAIN, D_MODEL), BF16))
        out_specs.append(pl.BlockSpec((tn, D_MODEL), lambda i, j: (j, 0)))
    outs = pl.pallas_call(
        functools.partial(_proj_kernel, tm=tm, emit=emit),
        out_shape=tuple(out_shape),
        grid=(m // tm, D_MAIN // tn),
        in_specs=[
            pl.BlockSpec((tm, D_MODEL), lambda i, j: (i, 0)),
            pl.BlockSpec((1, D_MODEL), lambda i, j: (0, 0)),
            _weight_spec(w, layer, (tn, D_MODEL), lambda i, j: (j, 0)),
            pl.BlockSpec((None, n_gate, D_MODEL), lambda i, j: (layer, D_MAIN // n_gate, 0)),
            pl.BlockSpec((1, LANES), lambda i, j: (0, 0)),
            pl.BlockSpec((1, LANES), lambda i, j: (0, 0)),
        ],
        out_specs=tuple(out_specs),
        scratch_shapes=[pltpu.VMEM((tm, D_MODEL), BF16), pltpu.VMEM((LANES, D_MODEL), BF16)],
        compiler_params=pltpu.CompilerParams(
            dimension_semantics=("arbitrary", "arbitrary"), vmem_limit_bytes=VMEM_LIMIT),
        name="norm_proj",
    )(x, nw, w, w_in_t, alog, dtb)
    return outs if emit else (*outs, w)


def _outproj_kernel(y_ref, w_ref, x_ref, o_ref, w16_ref=None):
    o_ref[...] = x_ref[...] + _dot(y_ref[...], _bf16_weight_ref(w_ref, w16_ref)[...])


def _outproj(y, w, x, *, layer, tm, tn):
    m = x.shape[0]
    emit = _emits_bf16(w, m, tm)
    out_shape = [jax.ShapeDtypeStruct((m, D_MODEL), F32)]
    out_specs = [pl.BlockSpec((tm, tn), lambda i, j: (i, j))]
    if emit:
        out_shape.append(jax.ShapeDtypeStruct((D_MODEL, D_MODEL), BF16))
        out_specs.append(pl.BlockSpec((D_MODEL, tn), lambda i, j: (0, j)))
    outs = pl.pallas_call(
        _outproj_kernel,
        out_shape=tuple(out_shape),
        grid=(m // tm, D_MODEL // tn),
        in_specs=[
            pl.BlockSpec((tm, D_MODEL), lambda i, j: (i, 0)),
            _weight_spec(w, layer, (D_MODEL, tn), lambda i, j: (0, j)),
            pl.BlockSpec((tm, tn), lambda i, j: (i, j)),
        ],
        out_specs=tuple(out_specs),
        compiler_params=pltpu.CompilerParams(
            dimension_semantics=("arbitrary", "arbitrary"), vmem_limit_bytes=VMEM_LIMIT),
        name="out_proj",
    )(y, w, x)
    return outs if emit else (*outs, w)


def _ffn_kernel(x_ref, nw_ref, wu_ref, wd_ref, fw_ref, o_ref, *rest, tm, final, emit):
    wu16_ref, wd16_ref, h_scr = rest if emit else (None, None) + rest
    f = pl.program_id(1)
    wu = _bf16_weight_ref(wu_ref, wu16_ref)
    wd = _bf16_weight_ref(wd_ref, wd16_ref)

    def add_mlp(h, rows, start):
        for c in range(0, wu.shape[1], HIDDEN_SUB):
            up = jnp.maximum(_dot(h, wu[:, c:c + HIDDEN_SUB]), 0.0)
            part = _dot((up * up).astype(BF16), wd[c:c + HIDDEN_SUB, :])
            if c == 0 and start is not None:
                o_ref[rows, :] = start + part
            else:
                o_ref[rows, :] += part

    @pl.when(f == 0)
    def _():
        for r in range(0, tm, ROW_SUB):
            rows = slice(r, r + ROW_SUB)
            x = x_ref[rows, :]
            h = (_rms(x) * nw_ref[...]).astype(BF16)
            h_scr[rows, :] = h
            add_mlp(h, rows, x)

    @pl.when(f > 0)
    def _():
        add_mlp(h_scr[...], slice(None), None)

    if final:
        @pl.when(f == pl.num_programs(1) - 1)
        def _():
            for r in range(0, tm, ROW_SUB):
                rows = slice(r, r + ROW_SUB)
                o_ref[rows, :] = _rms(o_ref[rows, :]) * fw_ref[...]


def _ffn(x, nw, wu, wd, fw, *, layer, tm, tf, final):
    m = x.shape[0]
    emit = _emits_bf16(wu, m, tm)
    assert (wd.dtype == F32) == emit
    out_shape = [jax.ShapeDtypeStruct((m, D_MODEL), F32)]
    out_specs = [pl.BlockSpec((tm, D_MODEL), lambda i, f: (i, 0))]
    if emit:
        out_shape += [jax.ShapeDtypeStruct((D_MODEL, D_FF), BF16), jax.ShapeDtypeStruct((D_FF, D_MODEL), BF16)]
        out_specs += [pl.BlockSpec((D_MODEL, tf), lambda i, f: (0, f)), pl.BlockSpec((tf, D_MODEL), lambda i, f: (f, 0))]
    outs = pl.pallas_call(
        functools.partial(_ffn_kernel, tm=tm, final=final, emit=emit),
        out_shape=tuple(out_shape),
        grid=(m // tm, D_FF // tf),
        in_specs=[
            pl.BlockSpec((tm, D_MODEL), lambda i, f: (i, 0)),
            pl.BlockSpec((1, D_MODEL), lambda i, f: (0, 0)),
            _weight_spec(wu, layer, (D_MODEL, tf), lambda i, f: (0, f)),
            _weight_spec(wd, layer, (tf, D_MODEL), lambda i, f: (f, 0)),
            pl.BlockSpec((1, D_MODEL), lambda i, f: (0, 0)),
        ],
        out_specs=tuple(out_specs),
        scratch_shapes=[pltpu.VMEM((tm, D_MODEL), BF16)],
        compiler_params=pltpu.CompilerParams(
            dimension_semantics=("arbitrary", "arbitrary"), vmem_limit_bytes=VMEM_LIMIT),
        name="ffn",
    )(x, nw, wu, wd, fw)
    return outs if emit else (*outs, wu, wd)


def _group_masks(n, group):
    row = lax.broadcasted_iota(jnp.int32, (n, n), 0)
    col = lax.broadcasted_iota(jnp.int32, (n, n), 1)
    same = (row // group) == (col // group)
    incl = same & (row >= col)
    strict = same & (row > col)
    eye = jnp.where(row == col, 1.0, 0.0).astype(F32)
    return row, col, incl, strict, eye


def _merge_masks(row, col, base, group):
    masks = []
    b = base
    while b < group:
        masks.append(((row // (2 * b)) == (col // (2 * b))) & ((row // b) != (col // b)))
        b *= 2
    return masks


def _each(f, *lists):
    return [f(*args) for args in zip(*lists)]


def _unit_lower_inverse(a, eye, base, base_mask, merge_masks):
    n1 = a if base_mask is None else _each(lambda t: jnp.where(base_mask, t, 0.0), a)
    n1b = _each(lambda t: t.astype(BF16), n1)
    n2 = _each(_dot, n1b, n1b)
    n2b = _each(lambda t: t.astype(BF16), n2)
    n3 = _each(_dot, n1b, n2b)
    x = _each(lambda p1, p2, p3: eye - p1 + p2 - p3, n1, n2, n3)
    if base == 8:
        n4b = _each(lambda t: _dot(t, t).astype(BF16), n2b)
        x = _each(lambda t, p4: t + _dot(t.astype(BF16), p4), x, n4b)
    for m in merge_masks:
        xb = _each(lambda t: t.astype(BF16), x)
        xl = _each(lambda tb, t: _dot(tb, jnp.where(m, t, 0.0).astype(BF16)).astype(BF16), xb, a)
        x = _each(lambda t, l, tb: t - _dot(l, tb), x, xl, xb)
    return x


def _chunk_prepare(q, k, v, gcol, bcol, grow, incl, strict, inverse):
    decay = _each(lambda gc, gr: jnp.exp(jnp.where(incl, gc - gr, -jnp.inf)), gcol, grow)
    kb = _each(lambda t: t.astype(BF16), k)
    kk = _each(_dot_nt, kb, kb)
    qk = _each(lambda t, tb: _dot_nt(t.astype(BF16), tb), q, kb)
    a = _each(lambda b, d, m: jnp.where(strict, b * d * m, 0.0), bcol, decay, kk)
    t = inverse(a)
    eg = _each(jnp.exp, gcol)
    rhs = _each(lambda b, e, vv, kx: jnp.concatenate([b * vv, (b * e) * kx], axis=1).astype(BF16), bcol, eg, v, k)
    sol = _each(lambda tt, r: _dot(tt.astype(BF16), r), t, rhs)
    return (_each(lambda s: s[:, :DK], sol), _each(lambda s: s[:, DK:], sol),
            _each(lambda qq, e: qq * e, q, eg), _each(lambda m, d: m * d, qk, decay))


def _gated_out(o, z, dnw):
    return (_rms(o) * dnw * _silu(z)).astype(BF16)


def _conv_a_group(p_ref, gi, caw_ref, canw_ref, shift):
    cols = slice(gi * LANES, (gi + 1) * LANES)
    c_cols = slice(D_CONV + gi * LANES, D_CONV + (gi + 1) * LANES)
    h_cols = slice(2 * D_CONV + gi * LANES, 2 * D_CONV + (gi + 1) * LANES)
    b_a = p_ref[:, cols]
    u = p_ref[:, c_cols] * p_ref[:, h_cols]
    conv = (caw_ref[2:3, cols] * u + caw_ref[1:2, cols] * shift(u, 1, cols)
            + caw_ref[0:1, cols] * shift(u, 2, cols))
    y = b_a * conv
    return u, (_rms(y) * canw_ref[:, cols]).astype(BF16)


def _conv_qkv_group(p_ref, gi, cqw_ref, shift):
    cols = slice(gi * LANES, (gi + 1) * LANES)
    x_cols = slice(QKV_OFF + gi * LANES, QKV_OFF + (gi + 1) * LANES)
    x = p_ref[:, x_cols]
    conv = (cqw_ref[3:4, cols] * x + cqw_ref[2:3, cols] * shift(x, 1, cols)
            + cqw_ref[1:2, cols] * shift(x, 2, cols) + cqw_ref[0:1, cols] * shift(x, 3, cols))
    c = _silu(conv)
    if gi < 2 * N_HEADS:
        c = c * lax.rsqrt(jnp.sum(c * c, axis=-1, keepdims=True) + EPS)
        if gi < N_HEADS:
            c = c * (DK ** -0.5)
    return x, c


def _mixer_prompt_kernel(p_ref, gb_ref, caw_ref, canw_ref, cqw_ref, dnw_ref,
                         y_ref, lasta_ref, lastq_ref, sfin_ref,
                         qkv_scr, ua_carry, xq_carry, st_scr, *, rb):
    nb = pl.program_id(1)

    @pl.when(nb == 0)
    def _():
        ua_carry[...] = jnp.zeros_like(ua_carry)
        xq_carry[...] = jnp.zeros_like(xq_carry)
        st_scr[...] = jnp.zeros_like(st_scr)

    row8 = lax.broadcasted_iota(jnp.int32, (SUBLANES, LANES), 0)

    def make_shift(carry_ref):
        def shift(x, s, cols):
            rolled = pltpu.roll(x, s, 0)
            prev = pltpu.roll(carry_ref[:, cols], s, 0)
            first = jnp.where(row8 < s, prev, rolled[:SUBLANES])
            return jnp.concatenate([first, rolled[SUBLANES:]], axis=0)
        return shift

    shift_a = make_shift(ua_carry)
    for gi in range(D_CONV // LANES):
        cols = slice(gi * LANES, (gi + 1) * LANES)
        u, y = _conv_a_group(p_ref, gi, caw_ref, canw_ref, shift_a)
        ua_carry[:, cols] = u[rb - SUBLANES:]
        y_ref[:, cols] = y

    shift_q = make_shift(xq_carry)
    for gi in range(3 * D_DN // LANES):
        cols = slice(gi * LANES, (gi + 1) * LANES)
        x, c = _conv_qkv_group(p_ref, gi, cqw_ref, shift_q)
        xq_carry[:, cols] = x[rb - SUBLANES:]
        qkv_scr[:, cols] = c

    row, col, incl, strict, eye = _group_masks(CHUNK, CHUNK)
    base_mask = (row // 8) == (col // 8)
    merges = _merge_masks(row, col, 8, CHUNK)
    inverse = functools.partial(_unit_lower_inverse, eye=eye, base=8, base_mask=base_mask, merge_masks=merges)
    rowc = lax.broadcasted_iota(jnp.int32, (CHUNK, LANES), 0)

    heads = list(range(N_HEADS))
    chunks = list(range(rb // CHUNK))
    q, k, v, gcol, bcol, grow = [], [], [], [], [], []
    for c in chunks:
        rows = slice(c * CHUNK, (c + 1) * CHUNK)
        gb = gb_ref[rows, :]
        gcum = gb
        s = 1
        while s < CHUNK:
            gcum = gcum + jnp.where(rowc >= s, pltpu.roll(gcum, s, 0), 0.0)
            s *= 2
        gcum_t = gcum.T
        q += [qkv_scr[rows, h * DK:(h + 1) * DK] for h in heads]
        k += [qkv_scr[rows, D_DN + h * DK:D_DN + (h + 1) * DK] for h in heads]
        v += [qkv_scr[rows, 2 * D_DN + h * DK:2 * D_DN + (h + 1) * DK] for h in heads]
        gcol += [gcum[:, h:h + 1] for h in heads]
        bcol += [gb[:, N_HEADS + h:N_HEADS + h + 1] for h in heads]
        grow += [gcum_t[h:h + 1, :] for h in heads]
    ub, w, qd, p = _chunk_prepare(q, k, v, gcol, bcol, grow, incl, strict, inverse)
    ke = _each(lambda a, g: (a * jnp.exp(g[CHUNK - 1:CHUNK, :] - g)).astype(BF16), k, gcol)
    lhs = _each(lambda a, b: jnp.concatenate([a, b], axis=0).astype(BF16), w, qd)
    p16 = _each(lambda a: a.astype(BF16), p)
    g_end = _each(lambda g: jnp.exp(g[CHUNK - 1:CHUNK, :]), gcol)

    st = [st_scr[h] for h in heads]
    for c in chunks:
        rows = slice(c * CHUNK, (c + 1) * CHUNK)
        sl = slice(c * N_HEADS, (c + 1) * N_HEADS)
        ws = _each(lambda a, s: _dot(a, s.astype(BF16)), lhs[sl], st)
        u16 = _each(lambda a, b: (a - b[:CHUNK]).astype(BF16), ub[sl], ws)
        o = _each(lambda a, b, c: a[CHUNK:] + _dot(b, c), ws, p16[sl], u16)
        st = _each(lambda g, s, a, b: g * s + _dot_tn(a, b), g_end[sl], st, ke[sl], u16)
        for h in heads:
            y_ref[rows, D_CONV + h * DK:D_CONV + (h + 1) * DK] = _gated_out(
                o[h], p_ref[rows, Z_OFF + h * DK:Z_OFF + (h + 1) * DK], dnw_ref[...])
    for h in heads:
        st_scr[h] = st[h]

    @pl.when(nb == pl.num_programs(1) - 1)
    def _():
        lasta_ref[0] = ua_carry[...]
        lastq_ref[0] = xq_carry[...]
        for h in range(N_HEADS):
            sfin_ref[0, h] = st_scr[h].T


def _mixer_prompt(proj, gb, caw, canw, cqw, dnw, *, n_seq, seq_len, rb=MIXER_ROWS):
    nblk = seq_len // rb
    m = n_seq * seq_len
    full = lambda shape: pl.BlockSpec(shape, lambda b, n: (0,) * len(shape))
    return pl.pallas_call(
        functools.partial(_mixer_prompt_kernel, rb=rb),
        out_shape=(
            jax.ShapeDtypeStruct((m, D_MODEL), BF16),
            jax.ShapeDtypeStruct((n_seq, SUBLANES, D_CONV), F32),
            jax.ShapeDtypeStruct((n_seq, SUBLANES, 3 * D_DN), F32),
            jax.ShapeDtypeStruct((n_seq, N_HEADS, DK, DK), F32),
        ),
        grid=(n_seq, nblk),
        in_specs=[
            pl.BlockSpec((rb, D_MAIN), lambda b, n: (b * nblk + n, 0)),
            pl.BlockSpec((rb, LANES), lambda b, n: (b * nblk + n, 0)),
            full((SUBLANES, D_CONV)),
            full((1, D_CONV)),
            full((SUBLANES, 3 * D_DN)),
            full((1, DK)),
        ],
        out_specs=(
            pl.BlockSpec((rb, D_MODEL), lambda b, n: (b * nblk + n, 0)),
            pl.BlockSpec((1, SUBLANES, D_CONV), lambda b, n: (b, 0, 0)),
            pl.BlockSpec((1, SUBLANES, 3 * D_DN), lambda b, n: (b, 0, 0)),
            pl.BlockSpec((1, N_HEADS, DK, DK), lambda b, n: (b, 0, 0, 0)),
        ),
        scratch_shapes=[
            pltpu.VMEM((rb, 3 * D_DN), F32),
            pltpu.VMEM((SUBLANES, D_CONV), F32),
            pltpu.VMEM((SUBLANES, 3 * D_DN), F32),
            pltpu.VMEM((N_HEADS, DK, DK), F32),
        ],
        compiler_params=pltpu.CompilerParams(
            dimension_semantics=("arbitrary", "arbitrary"), vmem_limit_bytes=VMEM_LIMIT),
        name="mixer_prompt",
    )(proj, gb, caw, canw, cqw, dnw)


SEQ_PER_STEP = 8
ROWS_PER_STEP = SEQ_PER_STEP * SAMPLE_LEN


def _mixer_sample_kernel(p_ref, gb_ref, bufa_ref, bufq_ref, caw_ref, canw_ref, cqw_ref, dnw_ref, s_ref, prev_ref,
                         y_ref, newa_ref, newq_ref, snew_ref,
                         w_scr, qd_scr, ub_scr, ke_scr, p_scr, u_scr, qs_scr, g_scr, ua_scr, bxa_scr, xq_scr, bxq_scr):
    del prev_ref
    j = pl.program_id(1)
    n = CHUNK
    seqs = n // SAMPLE_LEN

    def token_rows(t):
        return pl.ds(t, seqs, stride=SAMPLE_LEN)

    @pl.when(j == 0)
    def _():
        rown = lax.broadcasted_iota(jnp.int32, (n, LANES), 0)
        tok = rown % SAMPLE_LEN

        def make_shift(buf_ref, bx_scr):
            first = SAMPLE_LEN - buf_ref.shape[0]
            bx_scr[...] = jnp.zeros_like(bx_scr)
            for m in range(buf_ref.shape[0]):
                for g in range(bx_scr.shape[0]):
                    bx_scr[g, token_rows(first + m), :] = buf_ref[m, :, g * LANES:(g + 1) * LANES]

            def shift(x, s, cols):
                return jnp.where(tok >= s, pltpu.roll(x, s, 0),
                                 pltpu.roll(bx_scr[cols.start // LANES], n - (SAMPLE_LEN - s), 0))
            return shift

        def emit_tail(raw_scr, new_ref):
            first = SAMPLE_LEN - new_ref.shape[0]
            for m in range(new_ref.shape[0]):
                for g in range(raw_scr.shape[0]):
                    new_ref[m, :, g * LANES:(g + 1) * LANES] = raw_scr[g, token_rows(first + m), :]

        shift_a = make_shift(bufa_ref, bxa_scr)
        for gi in range(D_CONV // LANES):
            cols = slice(gi * LANES, (gi + 1) * LANES)
            u, y = _conv_a_group(p_ref, gi, caw_ref, canw_ref, shift_a)
            ua_scr[gi] = u
            y_ref[:, cols] = y
        emit_tail(ua_scr, newa_ref)

        shift_q = make_shift(bufq_ref, bxq_scr)
        qkv = [None] * (3 * N_HEADS)
        for gi in range(3 * D_DN // LANES):
            xq_scr[gi], qkv[gi] = _conv_qkv_group(p_ref, gi, cqw_ref, shift_q)
        emit_tail(xq_scr, newq_ref)

        row, col, incl, strict, eye = _group_masks(n, SAMPLE_LEN)
        inverse = functools.partial(_unit_lower_inverse, eye=eye, base=SAMPLE_LEN, base_mask=None, merge_masks=[])

        gb = gb_ref[...]
        gcum = gb
        s = 1
        while s < SAMPLE_LEN:
            gcum = gcum + jnp.where(tok >= s, pltpu.roll(gcum, s, 0), 0.0)
            s *= 2
        glast = gcum
        for back in range(1, SAMPLE_LEN):
            glast = jnp.where(tok == SAMPLE_LEN - 1 - back, pltpu.roll(gcum, n - back, 0), glast)
        g_scr[...] = gcum
        gcum_t = gcum.T
        e_end = jnp.exp(glast - gcum)
        heads = list(range(N_HEADS))
        gcol = [gcum[:, h:h + 1] for h in heads]
        bcol = [gb[:, N_HEADS + h:N_HEADS + h + 1] for h in heads]
        grow = [gcum_t[h:h + 1, :] for h in heads]
        ub, w, qd, p = _chunk_prepare(qkv[:N_HEADS], qkv[N_HEADS:2 * N_HEADS], qkv[2 * N_HEADS:],
                                      gcol, bcol, grow, incl, strict, inverse)
        for h in heads:
            hc = slice(h * DK, (h + 1) * DK)
            ub_scr[:, hc] = ub[h]
            w_scr[:, hc] = w[h]
            qd_scr[:, hc] = qd[h]
            ke_scr[:, hc] = qkv[N_HEADS + h] * e_end[:, h:h + 1]
            p_scr[h] = p[h]
        u_scr[...] = jnp.zeros_like(u_scr)

    r0 = pl.multiple_of(j * ROWS_PER_STEP, ROWS_PER_STEP)
    row16 = lax.broadcasted_iota(jnp.int32, (2 * SUBLANES, LANES), 0)
    first_of_pair = (row16 % SUBLANES) < SAMPLE_LEN
    coln = lax.broadcasted_iota(jnp.int32, (DK, n), 1)
    heads = list(range(N_HEADS))
    hcs = [slice(h * DK, (h + 1) * DK) for h in heads]
    for h in heads:
        for t in range(ROWS_PER_STEP // SUBLANES):
            rows = pl.ds(pl.multiple_of(r0 + t * SUBLANES, SUBLANES), SUBLANES)
            lhs = jnp.concatenate([w_scr[rows, hcs[h]], qd_scr[rows, hcs[h]]], axis=0).astype(BF16)
            da = _dot_nt(lhs, s_ref[2 * t, h].astype(BF16))
            db = _dot_nt(lhs, s_ref[2 * t + 1, h].astype(BF16))
            ws = jnp.where(first_of_pair, da, db)
            u_scr[rows, hcs[h]] = ub_scr[rows, hcs[h]] - ws[:SUBLANES]
            qs_scr[rows, hcs[h]] = ws[SUBLANES:]
    u_h = [u_scr[:, hc] for hc in hcs]
    u_t = _each(lambda t: t.T.astype(BF16), u_h)
    ke16 = [ke_scr[:, hc].astype(BF16) for hc in hcs]
    for sq in range(SEQ_PER_STEP):
        seq_cols = jnp.where(coln // SAMPLE_LEN == j * SEQ_PER_STEP + sq, 1.0, 0.0).astype(BF16)
        g_end = jnp.exp(g_scr[pl.ds(r0 + sq * SAMPLE_LEN + SAMPLE_LEN - 1, 1), :])
        upd = _each(lambda t, ke: _dot(t * seq_cols, ke), u_t, ke16)
        for h in heads:
            snew_ref[sq, h] = g_end[:, h:h + 1] * s_ref[sq, h] + upd[h]
    rows = pl.ds(r0, ROWS_PER_STEP)
    o = [qs_scr[rows, hcs[h]] + _dot(p_scr[h, rows, :].astype(BF16), u_h[h].astype(BF16)) for h in heads]
    for h in heads:
        y_ref[rows, D_CONV + h * DK:D_CONV + (h + 1) * DK] = _gated_out(
            o[h], p_ref[rows, Z_OFF + h * DK:Z_OFF + (h + 1) * DK], dnw_ref[...])


def _mixer_sample(proj, gb, bufa, bufq, caw, canw, cqw, dnw, state, new_state, *, layer):
    m = proj.shape[0]
    n = CHUNK
    seqs = n // SAMPLE_LEN
    n_seq = m // SAMPLE_LEN
    steps = n // ROWS_PER_STEP
    full = lambda shape: pl.BlockSpec(shape, lambda i, j: (0,) * len(shape))
    rowblk = lambda width: pl.BlockSpec((n, width), lambda i, j: (i, 0))
    sblk = pl.BlockSpec((None, SEQ_PER_STEP, N_HEADS, DK, DK), lambda i, j: (layer, i * steps + j, 0, 0, 0))
    la, lq = bufa.shape[1], bufq.shape[1]
    prev_index = 9
    return pl.pallas_call(
        _mixer_sample_kernel,
        out_shape=(
            jax.ShapeDtypeStruct((m, D_MODEL), BF16),
            jax.ShapeDtypeStruct((la, n_seq, D_CONV), F32),
            jax.ShapeDtypeStruct((lq, n_seq, 3 * D_DN), F32),
            jax.ShapeDtypeStruct(state.shape, F32),
        ),
        grid=(m // n, steps),
        in_specs=[
            rowblk(D_MAIN), rowblk(LANES),
            pl.BlockSpec((None, la, seqs, D_CONV), lambda i, j: (layer, 0, i, 0)),
            pl.BlockSpec((None, lq, seqs, 3 * D_DN), lambda i, j: (layer, 0, i, 0)),
            full((SUBLANES, D_CONV)), full((1, D_CONV)), full((SUBLANES, 3 * D_DN)), full((1, DK)),
            sblk, pl.BlockSpec(memory_space=pl.ANY),
        ],
        input_output_aliases={prev_index: 3} if layer > 0 else {},
        out_specs=(rowblk(D_MODEL),
                   pl.BlockSpec((la, seqs, D_CONV), lambda i, j: (0, i, 0)),
                   pl.BlockSpec((lq, seqs, 3 * D_DN), lambda i, j: (0, i, 0)),
                   sblk),
        scratch_shapes=[
            pltpu.VMEM((n, D_DN), F32),
            pltpu.VMEM((n, D_DN), F32),
            pltpu.VMEM((n, D_DN), F32),
            pltpu.VMEM((n, D_DN), F32),
            pltpu.VMEM((N_HEADS, n, n), F32),
            pltpu.VMEM((n, D_DN), F32),
            pltpu.VMEM((n, D_DN), F32),
            pltpu.VMEM((n, LANES), F32),
            pltpu.VMEM((D_CONV // LANES, n, LANES), F32),
            pltpu.VMEM((D_CONV // LANES, n, LANES), F32),
            pltpu.VMEM((3 * D_DN // LANES, n, LANES), F32),
            pltpu.VMEM((3 * D_DN // LANES, n, LANES), F32),
        ],
        compiler_params=pltpu.CompilerParams(
            dimension_semantics=("arbitrary", "arbitrary"), vmem_limit_bytes=VMEM_LIMIT),
        name="mixer_sample",
    )(proj, gb, bufa, bufq, caw, canw, cqw, dnw, state, new_state)


def _pad_rows(w):
    return jnp.pad(w, ((0, 0), (0, SUBLANES - w.shape[1]), (0, 0)))


def kernel(x_prompt, x_sample, state_conv_a, state_conv_qkv, state_delta, norm_mix_w, w_in,
           conv_a_w, conv_a_norm_w, conv_qkv_w, a_log, dt_bias, dn_norm_w, w_out,
           norm_ffn_w, w_up, w_down, final_norm_w):
    depth = w_in.shape[0]
    n_seq, seq_len, _ = x_prompt.shape
    n_dec, dec_len, _ = x_sample.shape
    assert dec_len == SAMPLE_LEN

    w_in_t = jnp.swapaxes(w_in, 1, 2)
    alog = jnp.pad(a_log, ((0, 0), (0, LANES - N_HEADS)))[:, None, :]
    dtb = jnp.pad(dt_bias, ((0, 0), (0, LANES - N_HEADS)))[:, None, :]
    caw, cqw = _pad_rows(conv_a_w), _pad_rows(conv_qkv_w)
    bufa = jnp.swapaxes(state_conv_a, 1, 2)
    bufq = jnp.swapaxes(state_conv_qkv, 1, 2)

    xp = x_prompt.reshape(n_seq * seq_len, D_MODEL)
    xs = x_sample.reshape(n_dec * dec_len, D_MODEL)
    tp, ts = PROMPT_TILES, SAMPLE_TILES
    fw = final_norm_w[None, :]
    conv_a_p, conv_q_p, delta_p, conv_a_s, conv_q_s = [], [], [], [], []
    delta_s = state_delta
    for l in range(depth):
        nmw, nfw = norm_mix_w[l][None, :], norm_ffn_w[l][None, :]
        canw, dnw = conv_a_norm_w[l][None, :], dn_norm_w[l][None, :]
        final = l == depth - 1
        proj, gb, w_main16 = _proj(xs, nmw, w_in_t, w_in_t, alog[l], dtb[l], layer=l,
                                   tm=ts["proj"][0], tn=ts["proj"][1])
        y, new_a, new_q, delta_s = _mixer_sample(proj, gb, bufa, bufq, caw[l], canw, cqw[l], dnw, state_delta,
                                                 delta_s, layer=l)
        xs, w_out16 = _outproj(y, w_out, xs, layer=l, tm=ts["out"][0], tn=ts["out"][1])
        xs, w_up16, w_down16 = _ffn(xs, nfw, w_up, w_down, fw, layer=l, tm=ts["ffn"][0], tf=ts["ffn"][1],
                                    final=final)
        conv_a_s.append(new_a)
        conv_q_s.append(new_q)
        proj, gb, _ = _proj(xp, nmw, w_main16, w_in_t, alog[l], dtb[l], layer=l, tm=tp["proj"][0], tn=tp["proj"][1])
        y, last_a, last_q, s_fin = _mixer_prompt(proj, gb, caw[l], canw, cqw[l], dnw, n_seq=n_seq, seq_len=seq_len)
        xp, _ = _outproj(y, w_out16, xp, layer=l, tm=tp["out"][0], tn=tp["out"][1])
        xp, _, _ = _ffn(xp, nfw, w_up16, w_down16, fw, layer=l, tm=tp["ffn"][0], tf=tp["ffn"][1], final=final)
        conv_a_p.append(last_a[:, SUBLANES - 2:])
        conv_q_p.append(last_q[:, SUBLANES - 3:])
        delta_p.append(s_fin)

    return (xp.reshape(n_seq, seq_len, D_MODEL), xs.reshape(n_dec, dec_len, D_MODEL),
            jnp.stack(conv_a_p), jnp.stack(conv_q_p), jnp.stack(delta_p),
            jnp.swapaxes(jnp.stack(conv_a_s), 1, 2), jnp.swapaxes(jnp.stack(conv_q_s), 1, 2), delta_s)
```

```python
import functools

import jax
import jax.numpy as jnp
from jax import lax
from jax.experimental import pallas as pl
from jax.experimental.pallas import tpu as pltpu

F32 = jnp.float32
BF16 = jnp.bfloat16

D_MODEL = 2048
D_CONV = 1024
D_DN = 1024
N_HEADS = 8
DK = 128
D_FF = 8192
D_MAIN = 3 * D_CONV + 4 * D_DN
QKV_OFF = 3 * D_CONV
Z_OFF = QKV_OFF + 3 * D_DN
EPS = 1e-6
LANES = 128
SUBLANES = 8
CHUNK = 128
SAMPLE_LEN = 4
VMEM_LIMIT = 56 * 1024 * 1024

PROMPT_TILES = dict(proj=(1024, 1792), out=(512, D_MODEL), ffn=(1024, 512))
SAMPLE_TILES = dict(proj=(512, 1024), out=(512, 1024), ffn=(512, 512))
ROW_SUB = 256
HIDDEN_SUB = 512
MIXER_ROWS = 256


def _dot(a, b):
    return jnp.dot(a, b, preferred_element_type=F32)


def _dot_nt(a, b):
    return lax.dot_general(a, b, (((1,), (1,)), ((), ())), preferred_element_type=F32)


def _dot_tn(a, b):
    return lax.dot_general(a, b, (((0,), (0,)), ((), ())), preferred_element_type=F32)


def _softplus(x):
    return jnp.maximum(x, 0.0) + jnp.log1p(jnp.exp(-jnp.abs(x)))


def _silu(x):
    return x * jax.nn.sigmoid(x)


def _rms(x):
    return x * lax.rsqrt(jnp.mean(x * x, axis=-1, keepdims=True) + EPS)


def _weight_spec(w, layer, block, index):
    if w.ndim == 2:
        return pl.BlockSpec(block, index, pipeline_mode=pl.Buffered(1) if block == w.shape else None)
    return pl.BlockSpec((None,) + block, lambda *g: (layer,) + index(*g))


def _bf16_weight_ref(w_ref, copy_ref):
    if copy_ref is None:
        return w_ref
    copy_ref[...] = w_ref[...].astype(BF16)
    return copy_ref


def _proj_kernel(x_ref, nw_ref, w_ref, wab_ref, alog_ref, dtb_ref, proj_ref, gb_ref, *rest, tm, emit):
    w16_ref, h_scr, wab_scr = rest if emit else (None,) + rest
    w = _bf16_weight_ref(w_ref, w16_ref)
    first = pl.program_id(1) == 0

    @pl.when(first)
    def _():
        wab_scr[...] = jnp.zeros_like(wab_scr)
        wab_scr[:2 * N_HEADS, :] = wab_ref[...].astype(BF16)
        for r in range(0, tm, ROW_SUB):
            rows = slice(r, r + ROW_SUB)
            h = (_rms(x_ref[rows, :]) * nw_ref[...]).astype(BF16)
            h_scr[rows, :] = h
            ab = _dot_nt(h, wab_scr[...])
            lane = lax.broadcasted_iota(jnp.int32, ab.shape, 1)
            g = -jnp.exp(alog_ref[...]) * _softplus(ab + dtb_ref[...])
            gb_ref[rows, :] = jnp.where(lane < N_HEADS, g, jax.nn.sigmoid(ab))
            proj_ref[rows, :] = _dot_nt(h, w[...])

    @pl.when(jnp.logical_not(first))
    def _():
        proj_ref[...] = _dot_nt(h_scr[...], w[...])


def _emits_bf16(w, m, tm):
    emit = w.dtype == F32
    assert not emit or m == tm
    return emit


def _proj(x, nw, w, w_in_t, alog, dtb, *, layer, tm, tn):
    m = x.shape[0]
    emit = _emits_bf16(w, m, tm)
    n_gate = 2 * N_HEADS
    out_shape = [jax.ShapeDtypeStruct((m, D_MAIN), F32), jax.ShapeDtypeStruct((m, LANES), F32)]
    out_specs = [pl.BlockSpec((tm, tn), lambda i, j: (i, j)), pl.BlockSpec((tm, LANES), lambda i, j: (i, 0))]
    if emit:
        out_shape.append(jax.ShapeDtypeStruct((D_MAIN, D_MODEL), BF16))
        out_specs.append(pl.BlockSpec((tn, D_MODEL), lambda i, j: (j, 0)))
    outs = pl.pallas_call(
        functools.partial(_proj_kernel, tm=tm, emit=emit),
        out_shape=tuple(out_shape),
        grid=(m // tm, D_MAIN // tn),
        in_specs=[
            pl.BlockSpec((tm, D_MODEL), lambda i, j: (i, 0)),
            pl.BlockSpec((1, D_MODEL), lambda i, j: (0, 0)),
            _weight_spec(w, layer, (tn, D_MODEL), lambda i, j: (j, 0)),
            pl.BlockSpec((None, n_gate, D_MODEL), lambda i, j: (layer, D_MAIN // n_gate, 0)),
            pl.BlockSpec((1, LANES), lambda i, j: (0, 0)),
            pl.BlockSpec((1, LANES), lambda i, j: (0, 0)),
        ],
        out_specs=tuple(out_specs),
        scratch_shapes=[pltpu.VMEM((tm, D_MODEL), BF16), pltpu.VMEM((LANES, D_MODEL), BF16)],
        compiler_params=pltpu.CompilerParams(
            dimension_semantics=("arbitrary", "arbitrary"), vmem_limit_bytes=VMEM_LIMIT),
        name="norm_proj",
    )(x, nw, w, w_in_t, alog, dtb)
    return outs if emit else (*outs, w)


def _outproj_kernel(y_ref, w_ref, x_ref, o_ref, w16_ref=None):
    o_ref[...] = x_ref[...] + _dot(y_ref[...], _bf16_weight_ref(w_ref, w16_ref)[...])


def _outproj(y, w, x, *, layer, tm, tn):
    m = x.shape[0]
    emit = _emits_bf16(w, m, tm)
    out_shape = [jax.ShapeDtypeStruct((m, D_MODEL), F32)]
    out_specs = [pl.BlockSpec((tm, tn), lambda i, j: (i, j))]
    if emit:
        out_shape.append(jax.ShapeDtypeStruct((D_MODEL, D_MODEL), BF16))
        out_specs.append(pl.BlockSpec((D_MODEL, tn), lambda i, j: (0, j)))
    outs = pl.pallas_call(
        _outproj_kernel,
        out_shape=tuple(out_shape),
        grid=(m // tm, D_MODEL // tn),
        in_specs=[
            pl.BlockSpec((tm, D_MODEL), lambda i, j: (i, 0)),
            _weight_spec(w, layer, (D_MODEL, tn), lambda i, j: (0, j)),
            pl.BlockSpec((tm, tn), lambda i, j: (i, j)),
        ],
        out_specs=tuple(out_specs),
        compiler_params=pltpu.CompilerParams(
            dimension_semantics=("arbitrary", "arbitrary"), vmem_limit_bytes=VMEM_LIMIT),
        name="out_proj",
    )(y, w, x)
    return outs if emit else (*outs, w)


def _ffn_kernel(x_ref, nw_ref, wu_ref, wd_ref, fw_ref, *rest, tm, final, emit, n_cast):
    cast_in, rest = rest[:n_cast], rest[n_cast:]
    o_ref, rest = rest[0], rest[1:]
    cast_out, rest = rest[len(rest) - 1 - n_cast:len(rest) - 1], rest[:len(rest) - 1 - n_cast] + rest[len(rest) - 1:]
    wu16_ref, wd16_ref, h_scr = rest if emit else (None, None) + rest
    f = pl.program_id(1)

    def cast_slabs():
        for src, dst in zip(cast_in, cast_out):
            dst[...] = src[...].astype(BF16)
    wu = _bf16_weight_ref(wu_ref, wu16_ref)
    wd = _bf16_weight_ref(wd_ref, wd16_ref)

    def add_mlp(h, rows, start):
        for c in range(0, wu.shape[1], HIDDEN_SUB):
            up = jnp.maximum(_dot(h, wu[:, c:c + HIDDEN_SUB]), 0.0)
            part = _dot((up * up).astype(BF16), wd[c:c + HIDDEN_SUB, :])
            if c == 0 and start is not None:
                o_ref[rows, :] = start + part
            else:
                o_ref[rows, :] += part

    @pl.when(f == 0)
    def _():
        cast_slabs()
        for r in range(0, tm, ROW_SUB):
            rows = slice(r, r + ROW_SUB)
            x = x_ref[rows, :]
            h = (_rms(x) * nw_ref[...]).astype(BF16)
            h_scr[rows, :] = h
            add_mlp(h, rows, x)

    @pl.when(f > 0)
    def _():
        cast_slabs()
        add_mlp(h_scr[...], slice(None), None)

    if final:
        @pl.when(f == pl.num_programs(1) - 1)
        def _():
            for r in range(0, tm, ROW_SUB):
                rows = slice(r, r + ROW_SUB)
                o_ref[rows, :] = _rms(o_ref[rows, :]) * fw_ref[...]


BF16_ROWS = 16


def _ffn(x, nw, wu, wd, fw, *, layer, tm, tf, final, cast=()):
    m = x.shape[0]
    emit = _emits_bf16(wu, m, tm)
    assert (wd.dtype == F32) == emit
    n_f = D_FF // tf
    steps = (m // tm) * n_f
    out_shape = [jax.ShapeDtypeStruct((m, D_MODEL), F32)]
    out_specs = [pl.BlockSpec((tm, D_MODEL), lambda i, f: (i, 0))]
    if emit:
        out_shape += [jax.ShapeDtypeStruct((D_MODEL, D_FF), BF16), jax.ShapeDtypeStruct((D_FF, D_MODEL), BF16)]
        out_specs += [pl.BlockSpec((D_MODEL, tf), lambda i, f: (0, f)), pl.BlockSpec((tf, D_MODEL), lambda i, f: (f, 0))]
    cast_specs = []
    for arr, rows in cast:
        per = -(-rows // steps)
        per = -(-per // BF16_ROWS) * BF16_ROWS
        assert rows % per == 0 and rows // per <= steps
        slab = lambda i, f, last=rows // per - 1: jnp.minimum(i * n_f + f, last)
        cast_specs.append(pl.BlockSpec((None, per, arr.shape[2]), lambda i, f, slab=slab: (layer + 1, slab(i, f), 0)))
        out_shape.append(jax.ShapeDtypeStruct((rows, arr.shape[2]), BF16))
        out_specs.append(pl.BlockSpec((per, arr.shape[2]), lambda i, f, slab=slab: (slab(i, f), 0)))
    outs = pl.pallas_call(
        functools.partial(_ffn_kernel, tm=tm, final=final, emit=emit, n_cast=len(cast)),
        out_shape=tuple(out_shape),
        grid=(m // tm, n_f),
        in_specs=[
            pl.BlockSpec((tm, D_MODEL), lambda i, f: (i, 0)),
            pl.BlockSpec((1, D_MODEL), lambda i, f: (0, 0)),
            _weight_spec(wu, layer, (D_MODEL, tf), lambda i, f: (0, f)),
            _weight_spec(wd, layer, (tf, D_MODEL), lambda i, f: (f, 0)),
            pl.BlockSpec((1, D_MODEL), lambda i, f: (0, 0)),
        ] + cast_specs,
        out_specs=tuple(out_specs),
        scratch_shapes=[pltpu.VMEM((tm, D_MODEL), BF16)],
        compiler_params=pltpu.CompilerParams(
            dimension_semantics=("arbitrary", "arbitrary"), vmem_limit_bytes=VMEM_LIMIT),
        name="ffn",
    )(x, nw, wu, wd, fw, *[arr for arr, _ in cast])
    if emit:
        return outs
    return (outs[0], wu, wd, *outs[1:])


def _group_masks(n, group):
    row = lax.broadcasted_iota(jnp.int32, (n, n), 0)
    col = lax.broadcasted_iota(jnp.int32, (n, n), 1)
    same = (row // group) == (col // group)
    incl = same & (row >= col)
    strict = same & (row > col)
    eye = jnp.where(row == col, 1.0, 0.0).astype(F32)
    return row, col, incl, strict, eye


def _merge_masks(row, col, base, group):
    masks = []
    b = base
    while b < group:
        masks.append(((row // (2 * b)) == (col // (2 * b))) & ((row // b) != (col // b)))
        b *= 2
    return masks


def _each(f, *lists):
    return [f(*args) for args in zip(*lists)]


def _unit_lower_inverse(a, eye, base, base_mask, merge_masks):
    n1 = a if base_mask is None else _each(lambda t: jnp.where(base_mask, t, 0.0), a)
    n1b = _each(lambda t: t.astype(BF16), n1)
    n2 = _each(_dot, n1b, n1b)
    n2b = _each(lambda t: t.astype(BF16), n2)
    n3 = _each(_dot, n1b, n2b)
    x = _each(lambda p1, p2, p3: eye - p1 + p2 - p3, n1, n2, n3)
    if base == 8:
        n4b = _each(lambda t: _dot(t, t).astype(BF16), n2b)
        x = _each(lambda t, p4: t + _dot(t.astype(BF16), p4), x, n4b)
    for m in merge_masks:
        xb = _each(lambda t: t.astype(BF16), x)
        xl = _each(lambda tb, t: _dot(tb, jnp.where(m, t, 0.0).astype(BF16)).astype(BF16), xb, a)
        x = _each(lambda t, l, tb: t - _dot(l, tb), x, xl, xb)
    return x


def _chunk_prepare(q, k, v, gcol, bcol, grow, incl, strict, inverse):
    decay = _each(lambda gc, gr: jnp.exp(jnp.where(incl, gc - gr, -jnp.inf)), gcol, grow)
    kb = _each(lambda t: t.astype(BF16), k)
    kk = _each(_dot_nt, kb, kb)
    qk = _each(lambda t, tb: _dot_nt(t.astype(BF16), tb), q, kb)
    a = _each(lambda b, d, m: jnp.where(strict, b * d * m, 0.0), bcol, decay, kk)
    t = inverse(a)
    eg = _each(jnp.exp, gcol)
    rhs = _each(lambda b, e, vv, kx: jnp.concatenate([b * vv, (b * e) * kx], axis=1).astype(BF16), bcol, eg, v, k)
    sol = _each(lambda tt, r: _dot(tt.astype(BF16), r), t, rhs)
    return (_each(lambda s: s[:, :DK], sol), _each(lambda s: s[:, DK:], sol),
            _each(lambda qq, e: qq * e, q, eg), _each(lambda m, d: m * d, qk, decay))


def _gated_out(o, z, dnw):
    return (_rms(o) * dnw * _silu(z)).astype(BF16)


def _conv_a_group(p_ref, gi, caw_ref, canw_ref, shift):
    cols = slice(gi * LANES, (gi + 1) * LANES)
    c_cols = slice(D_CONV + gi * LANES, D_CONV + (gi + 1) * LANES)
    h_cols = slice(2 * D_CONV + gi * LANES, 2 * D_CONV + (gi + 1) * LANES)
    b_a = p_ref[:, cols]
    u = p_ref[:, c_cols] * p_ref[:, h_cols]
    conv = (caw_ref[2:3, cols] * u + caw_ref[1:2, cols] * shift(u, 1, cols)
            + caw_ref[0:1, cols] * shift(u, 2, cols))
    y = b_a * conv
    return u, (_rms(y) * canw_ref[:, cols]).astype(BF16)


def _conv_qkv_group(p_ref, gi, cqw_ref, shift):
    cols = slice(gi * LANES, (gi + 1) * LANES)
    x_cols = slice(QKV_OFF + gi * LANES, QKV_OFF + (gi + 1) * LANES)
    x = p_ref[:, x_cols]
    conv = (cqw_ref[3:4, cols] * x + cqw_ref[2:3, cols] * shift(x, 1, cols)
            + cqw_ref[1:2, cols] * shift(x, 2, cols) + cqw_ref[0:1, cols] * shift(x, 3, cols))
    c = _silu(conv)
    if gi < 2 * N_HEADS:
        c = c * lax.rsqrt(jnp.sum(c * c, axis=-1, keepdims=True) + EPS)
        if gi < N_HEADS:
            c = c * (DK ** -0.5)
    return x, c


def _mixer_prompt_kernel(p_ref, gb_ref, caw_ref, canw_ref, cqw_ref, dnw_ref,
                         y_ref, lasta_ref, lastq_ref, sfin_ref,
                         qkv_scr, ua_carry, xq_carry, st_scr, *, rb):
    nb = pl.program_id(1)

    @pl.when(nb == 0)
    def _():
        ua_carry[...] = jnp.zeros_like(ua_carry)
        xq_carry[...] = jnp.zeros_like(xq_carry)
        st_scr[...] = jnp.zeros_like(st_scr)

    row8 = lax.broadcasted_iota(jnp.int32, (SUBLANES, LANES), 0)

    def make_shift(carry_ref):
        def shift(x, s, cols):
            rolled = pltpu.roll(x, s, 0)
            prev = pltpu.roll(carry_ref[:, cols], s, 0)
            first = jnp.where(row8 < s, prev, rolled[:SUBLANES])
            return jnp.concatenate([first, rolled[SUBLANES:]], axis=0)
        return shift

    shift_a = make_shift(ua_carry)
    for gi in range(D_CONV // LANES):
        cols = slice(gi * LANES, (gi + 1) * LANES)
        u, y = _conv_a_group(p_ref, gi, caw_ref, canw_ref, shift_a)
        ua_carry[:, cols] = u[rb - SUBLANES:]
        y_ref[:, cols] = y

    shift_q = make_shift(xq_carry)
    for gi in range(3 * D_DN // LANES):
        cols = slice(gi * LANES, (gi + 1) * LANES)
        x, c = _conv_qkv_group(p_ref, gi, cqw_ref, shift_q)
        xq_carry[:, cols] = x[rb - SUBLANES:]
        qkv_scr[:, cols] = c

    row, col, incl, strict, eye = _group_masks(CHUNK, CHUNK)
    base_mask = (row // 8) == (col // 8)
    merges = _merge_masks(row, col, 8, CHUNK)
    inverse = functools.partial(_unit_lower_inverse, eye=eye, base=8, base_mask=base_mask, merge_masks=merges)
    rowc = lax.broadcasted_iota(jnp.int32, (CHUNK, LANES), 0)

    heads = list(range(N_HEADS))
    chunks = list(range(rb // CHUNK))
    q, k, v, gcol, bcol, grow = [], [], [], [], [], []
    for c in chunks:
        rows = slice(c * CHUNK, (c + 1) * CHUNK)
        gb = gb_ref[rows, :]
        gcum = gb
        s = 1
        while s < CHUNK:
            gcum = gcum + jnp.where(rowc >= s, pltpu.roll(gcum, s, 0), 0.0)
            s *= 2
        gcum_t = gcum.T
        q += [qkv_scr[rows, h * DK:(h + 1) * DK] for h in heads]
        k += [qkv_scr[rows, D_DN + h * DK:D_DN + (h + 1) * DK] for h in heads]
        v += [qkv_scr[rows, 2 * D_DN + h * DK:2 * D_DN + (h + 1) * DK] for h in heads]
        gcol += [gcum[:, h:h + 1] for h in heads]
        bcol += [gb[:, N_HEADS + h:N_HEADS + h + 1] for h in heads]
        grow += [gcum_t[h:h + 1, :] for h in heads]
    ub, w, qd, p = _chunk_prepare(q, k, v, gcol, bcol, grow, incl, strict, inverse)
    ke = _each(lambda a, g: (a * jnp.exp(g[CHUNK - 1:CHUNK, :] - g)).astype(BF16), k, gcol)
    lhs = _each(lambda a, b: jnp.concatenate([a, b], axis=0).astype(BF16), w, qd)
    p16 = _each(lambda a: a.astype(BF16), p)
    g_end = _each(lambda g: jnp.exp(g[CHUNK - 1:CHUNK, :]), gcol)

    st = [st_scr[h] for h in heads]
    for c in chunks:
        rows = slice(c * CHUNK, (c + 1) * CHUNK)
        sl = slice(c * N_HEADS, (c + 1) * N_HEADS)
        ws = _each(lambda a, s: _dot(a, s.astype(BF16)), lhs[sl], st)
        u16 = _each(lambda a, b: (a - b[:CHUNK]).astype(BF16), ub[sl], ws)
        o = _each(lambda a, b, c: a[CHUNK:] + _dot(b, c), ws, p16[sl], u16)
        st = _each(lambda g, s, a, b: g * s + _dot_tn(a, b), g_end[sl], st, ke[sl], u16)
        for h in heads:
            y_ref[rows, D_CONV + h * DK:D_CONV + (h + 1) * DK] = _gated_out(
                o[h], p_ref[rows, Z_OFF + h * DK:Z_OFF + (h + 1) * DK], dnw_ref[...])
    for h in heads:
        st_scr[h] = st[h]

    @pl.when(nb == pl.num_programs(1) - 1)
    def _():
        lasta_ref[0] = ua_carry[...]
        lastq_ref[0] = xq_carry[...]
        for h in range(N_HEADS):
            sfin_ref[0, h] = st_scr[h].T


def _mixer_prompt(proj, gb, caw, canw, cqw, dnw, *, n_seq, seq_len, rb=MIXER_ROWS):
    nblk = seq_len // rb
    m = n_seq * seq_len
    full = lambda shape: pl.BlockSpec(shape, lambda b, n: (0,) * len(shape))
    return pl.pallas_call(
        functools.partial(_mixer_prompt_kernel, rb=rb),
        out_shape=(
            jax.ShapeDtypeStruct((m, D_MODEL), BF16),
            jax.ShapeDtypeStruct((n_seq, SUBLANES, D_CONV), F32),
            jax.ShapeDtypeStruct((n_seq, SUBLANES, 3 * D_DN), F32),
            jax.ShapeDtypeStruct((n_seq, N_HEADS, DK, DK), F32),
        ),
        grid=(n_seq, nblk),
        in_specs=[
            pl.BlockSpec((rb, D_MAIN), lambda b, n: (b * nblk + n, 0)),
            pl.BlockSpec((rb, LANES), lambda b, n: (b * nblk + n, 0)),
            full((SUBLANES, D_CONV)),
            full((1, D_CONV)),
            full((SUBLANES, 3 * D_DN)),
            full((1, DK)),
        ],
        out_specs=(
            pl.BlockSpec((rb, D_MODEL), lambda b, n: (b * nblk + n, 0)),
            pl.BlockSpec((1, SUBLANES, D_CONV), lambda b, n: (b, 0, 0)),
            pl.BlockSpec((1, SUBLANES, 3 * D_DN), lambda b, n: (b, 0, 0)),
            pl.BlockSpec((1, N_HEADS, DK, DK), lambda b, n: (b, 0, 0, 0)),
        ),
        scratch_shapes=[
            pltpu.VMEM((rb, 3 * D_DN), F32),
            pltpu.VMEM((SUBLANES, D_CONV), F32),
            pltpu.VMEM((SUBLANES, 3 * D_DN), F32),
            pltpu.VMEM((N_HEADS, DK, DK), F32),
        ],
        compiler_params=pltpu.CompilerParams(
            dimension_semantics=("arbitrary", "arbitrary"), vmem_limit_bytes=VMEM_LIMIT),
        name="mixer_prompt",
    )(proj, gb, caw, canw, cqw, dnw)


SEQ_PER_STEP = 8
ROWS_PER_STEP = SEQ_PER_STEP * SAMPLE_LEN


def _mixer_sample_kernel(p_ref, gb_ref, bufa_ref, bufq_ref, caw_ref, canw_ref, cqw_ref, dnw_ref, s_ref, prev_ref,
                         y_ref, newa_ref, newq_ref, snew_ref,
                         w_scr, qd_scr, ub_scr, ke_scr, p_scr, u_scr, qs_scr, g_scr, ua_scr, bxa_scr, xq_scr, bxq_scr):
    del prev_ref
    j = pl.program_id(1)
    n = CHUNK
    seqs = n // SAMPLE_LEN

    def token_rows(t):
        return pl.ds(t, seqs, stride=SAMPLE_LEN)

    @pl.when(j == 0)
    def _():
        rown = lax.broadcasted_iota(jnp.int32, (n, LANES), 0)
        tok = rown % SAMPLE_LEN

        def make_shift(buf_ref, bx_scr):
            first = SAMPLE_LEN - buf_ref.shape[0]
            bx_scr[...] = jnp.zeros_like(bx_scr)
            for m in range(buf_ref.shape[0]):
                for g in range(bx_scr.shape[0]):
                    bx_scr[g, token_rows(first + m), :] = buf_ref[m, :, g * LANES:(g + 1) * LANES]

            def shift(x, s, cols):
                return jnp.where(tok >= s, pltpu.roll(x, s, 0),
                                 pltpu.roll(bx_scr[cols.start // LANES], n - (SAMPLE_LEN - s), 0))
            return shift

        def emit_tail(raw_scr, new_ref):
            first = SAMPLE_LEN - new_ref.shape[0]
            for m in range(new_ref.shape[0]):
                for g in range(raw_scr.shape[0]):
                    new_ref[m, :, g * LANES:(g + 1) * LANES] = raw_scr[g, token_rows(first + m), :]

        shift_a = make_shift(bufa_ref, bxa_scr)
        for gi in range(D_CONV // LANES):
            cols = slice(gi * LANES, (gi + 1) * LANES)
            u, y = _conv_a_group(p_ref, gi, caw_ref, canw_ref, shift_a)
            ua_scr[gi] = u
            y_ref[:, cols] = y
        emit_tail(ua_scr, newa_ref)

        shift_q = make_shift(bufq_ref, bxq_scr)
        qkv = [None] * (3 * N_HEADS)
        for gi in range(3 * D_DN // LANES):
            xq_scr[gi], qkv[gi] = _conv_qkv_group(p_ref, gi, cqw_ref, shift_q)
        emit_tail(xq_scr, newq_ref)

        row, col, incl, strict, eye = _group_masks(n, SAMPLE_LEN)
        inverse = functools.partial(_unit_lower_inverse, eye=eye, base=SAMPLE_LEN, base_mask=None, merge_masks=[])

        gb = gb_ref[...]
        gcum = gb
        s = 1
        while s < SAMPLE_LEN:
            gcum = gcum + jnp.where(tok >= s, pltpu.roll(gcum, s, 0), 0.0)
            s *= 2
        glast = gcum
        for back in range(1, SAMPLE_LEN):
            glast = jnp.where(tok == SAMPLE_LEN - 1 - back, pltpu.roll(gcum, n - back, 0), glast)
        g_scr[...] = gcum
        gcum_t = gcum.T
        e_end = jnp.exp(glast - gcum)
        heads = list(range(N_HEADS))
        gcol = [gcum[:, h:h + 1] for h in heads]
        bcol = [gb[:, N_HEADS + h:N_HEADS + h + 1] for h in heads]
        grow = [gcum_t[h:h + 1, :] for h in heads]
        ub, w, qd, p = _chunk_prepare(qkv[:N_HEADS], qkv[N_HEADS:2 * N_HEADS], qkv[2 * N_HEADS:],
                                      gcol, bcol, grow, incl, strict, inverse)
        for h in heads:
            hc = slice(h * DK, (h + 1) * DK)
            ub_scr[:, hc] = ub[h]
            w_scr[:, hc] = w[h]
            qd_scr[:, hc] = qd[h]
            ke_scr[:, hc] = qkv[N_HEADS + h] * e_end[:, h:h + 1]
            p_scr[h] = p[h]
        u_scr[...] = jnp.zeros_like(u_scr)

    r0 = pl.multiple_of(j * ROWS_PER_STEP, ROWS_PER_STEP)
    row16 = lax.broadcasted_iota(jnp.int32, (2 * SUBLANES, LANES), 0)
    first_of_pair = (row16 % SUBLANES) < SAMPLE_LEN
    coln = lax.broadcasted_iota(jnp.int32, (DK, n), 1)
    heads = list(range(N_HEADS))
    hcs = [slice(h * DK, (h + 1) * DK) for h in heads]
    for h in heads:
        for t in range(ROWS_PER_STEP // SUBLANES):
            rows = pl.ds(pl.multiple_of(r0 + t * SUBLANES, SUBLANES), SUBLANES)
            lhs = jnp.concatenate([w_scr[rows, hcs[h]], qd_scr[rows, hcs[h]]], axis=0).astype(BF16)
            da = _dot_nt(lhs, s_ref[2 * t, h].astype(BF16))
            db = _dot_nt(lhs, s_ref[2 * t + 1, h].astype(BF16))
            ws = jnp.where(first_of_pair, da, db)
            u_scr[rows, hcs[h]] = ub_scr[rows, hcs[h]] - ws[:SUBLANES]
            qs_scr[rows, hcs[h]] = ws[SUBLANES:]
    u_h = [u_scr[:, hc] for hc in hcs]
    u_t = _each(lambda t: t.T.astype(BF16), u_h)
    ke16 = [ke_scr[:, hc].astype(BF16) for hc in hcs]
    for sq in range(SEQ_PER_STEP):
        seq_cols = jnp.where(coln // SAMPLE_LEN == j * SEQ_PER_STEP + sq, 1.0, 0.0).astype(BF16)
        g_end = jnp.exp(g_scr[pl.ds(r0 + sq * SAMPLE_LEN + SAMPLE_LEN - 1, 1), :])
        upd = _each(lambda t, ke: _dot(t * seq_cols, ke), u_t, ke16)
        for h in heads:
            snew_ref[sq, h] = g_end[:, h:h + 1] * s_ref[sq, h] + upd[h]
    rows = pl.ds(r0, ROWS_PER_STEP)
    o = [qs_scr[rows, hcs[h]] + _dot(p_scr[h, rows, :].astype(BF16), u_h[h].astype(BF16)) for h in heads]
    for h in heads:
        y_ref[rows, D_CONV + h * DK:D_CONV + (h + 1) * DK] = _gated_out(
            o[h], p_ref[rows, Z_OFF + h * DK:Z_OFF + (h + 1) * DK], dnw_ref[...])


def _mixer_sample(proj, gb, bufa, bufq, caw, canw, cqw, dnw, state, new_state, *, layer):
    m = proj.shape[0]
    n = CHUNK
    seqs = n // SAMPLE_LEN
    n_seq = m // SAMPLE_LEN
    steps = n // ROWS_PER_STEP
    full = lambda shape: pl.BlockSpec(shape, lambda i, j: (0,) * len(shape))
    rowblk = lambda width: pl.BlockSpec((n, width), lambda i, j: (i, 0))
    sblk = pl.BlockSpec((None, SEQ_PER_STEP, N_HEADS, DK, DK), lambda i, j: (layer, i * steps + j, 0, 0, 0))
    la, lq = bufa.shape[1], bufq.shape[1]
    prev_index = 9
    return pl.pallas_call(
        _mixer_sample_kernel,
        out_shape=(
            jax.ShapeDtypeStruct((m, D_MODEL), BF16),
            jax.ShapeDtypeStruct((la, n_seq, D_CONV), F32),
            jax.ShapeDtypeStruct((lq, n_seq, 3 * D_DN), F32),
            jax.ShapeDtypeStruct(state.shape, F32),
        ),
        grid=(m // n, steps),
        in_specs=[
            rowblk(D_MAIN), rowblk(LANES),
            pl.BlockSpec((None, la, seqs, D_CONV), lambda i, j: (layer, 0, i, 0)),
            pl.BlockSpec((None, lq, seqs, 3 * D_DN), lambda i, j: (layer, 0, i, 0)),
            full((SUBLANES, D_CONV)), full((1, D_CONV)), full((SUBLANES, 3 * D_DN)), full((1, DK)),
            sblk, pl.BlockSpec(memory_space=pl.ANY),
        ],
        input_output_aliases={prev_index: 3} if layer > 0 else {},
        out_specs=(rowblk(D_MODEL),
                   pl.BlockSpec((la, seqs, D_CONV), lambda i, j: (0, i, 0)),
                   pl.BlockSpec((lq, seqs, 3 * D_DN), lambda i, j: (0, i, 0)),
                   sblk),
        scratch_shapes=[
            pltpu.VMEM((n, D_DN), F32),
            pltpu.VMEM((n, D_DN), F32),
            pltpu.VMEM((n, D_DN), F32),
            pltpu.VMEM((n, D_DN), F32),
            pltpu.VMEM((N_HEADS, n, n), F32),
            pltpu.VMEM((n, D_DN), F32),
            pltpu.VMEM((n, D_DN), F32),
            pltpu.VMEM((n, LANES), F32),
            pltpu.VMEM((D_CONV // LANES, n, LANES), F32),
            pltpu.VMEM((D_CONV // LANES, n, LANES), F32),
            pltpu.VMEM((3 * D_DN // LANES, n, LANES), F32),
            pltpu.VMEM((3 * D_DN // LANES, n, LANES), F32),
        ],
        compiler_params=pltpu.CompilerParams(
            dimension_semantics=("arbitrary", "arbitrary"), vmem_limit_bytes=VMEM_LIMIT),
        name="mixer_sample",
    )(proj, gb, bufa, bufq, caw, canw, cqw, dnw, state, new_state)


def _pad_rows(w):
    return jnp.pad(w, ((0, 0), (0, SUBLANES - w.shape[1]), (0, 0)))


def kernel(x_prompt, x_sample, state_conv_a, state_conv_qkv, state_delta, norm_mix_w, w_in,
           conv_a_w, conv_a_norm_w, conv_qkv_w, a_log, dt_bias, dn_norm_w, w_out,
           norm_ffn_w, w_up, w_down, final_norm_w):
    depth = w_in.shape[0]
    n_seq, seq_len, _ = x_prompt.shape
    n_dec, dec_len, _ = x_sample.shape
    assert dec_len == SAMPLE_LEN

    w_in_t = jnp.swapaxes(w_in, 1, 2)
    alog = jnp.pad(a_log, ((0, 0), (0, LANES - N_HEADS)))[:, None, :]
    dtb = jnp.pad(dt_bias, ((0, 0), (0, LANES - N_HEADS)))[:, None, :]
    caw, cqw = _pad_rows(conv_a_w), _pad_rows(conv_qkv_w)
    bufa = jnp.swapaxes(state_conv_a, 1, 2)
    bufq = jnp.swapaxes(state_conv_qkv, 1, 2)

    xp = x_prompt.reshape(n_seq * seq_len, D_MODEL)
    xs = x_sample.reshape(n_dec * dec_len, D_MODEL)
    tp, ts = PROMPT_TILES, SAMPLE_TILES
    fw = final_norm_w[None, :]
    conv_a_p, conv_q_p, delta_p, conv_a_s, conv_q_s = [], [], [], [], []
    delta_s = state_delta
    for l in range(depth):
        nmw, nfw = norm_mix_w[l][None, :], norm_ffn_w[l][None, :]
        canw, dnw = conv_a_norm_w[l][None, :], dn_norm_w[l][None, :]
        final = l == depth - 1
        w_main, w_o, w_u, w_d = (w_in_t, w_out, w_up, w_down) if l == 0 else next16
        proj, gb, w_main16 = _proj(xs, nmw, w_main, w_in_t, alog[l], dtb[l], layer=l,
                                   tm=ts["proj"][0], tn=ts["proj"][1])
        y, new_a, new_q, delta_s = _mixer_sample(proj, gb, bufa, bufq, caw[l], canw, cqw[l], dnw, state_delta,
                                                 delta_s, layer=l)
        xs, w_out16 = _outproj(y, w_o, xs, layer=l, tm=ts["out"][0], tn=ts["out"][1])
        xs, w_up16, w_down16 = _ffn(xs, nfw, w_u, w_d, fw, layer=l, tm=ts["ffn"][0], tf=ts["ffn"][1],
                                    final=final)
        conv_a_s.append(new_a)
        conv_q_s.append(new_q)
        proj, gb, _ = _proj(xp, nmw, w_main16, w_in_t, alog[l], dtb[l], layer=l, tm=tp["proj"][0], tn=tp["proj"][1])
        y, last_a, last_q, s_fin = _mixer_prompt(proj, gb, caw[l], canw, cqw[l], dnw, n_seq=n_seq, seq_len=seq_len)
        xp, _ = _outproj(y, w_out16, xp, layer=l, tm=tp["out"][0], tn=tp["out"][1])
        cast = () if final else ((w_in_t, D_MAIN), (w_out, D_MODEL), (w_up, D_MODEL), (w_down, D_FF))
        xp, _, _, *next16 = _ffn(xp, nfw, w_up16, w_down16, fw, layer=l, tm=tp["ffn"][0], tf=tp["ffn"][1],
                                 final=final, cast=cast)
        conv_a_p.append(last_a[:, SUBLANES - 2:])
        conv_q_p.append(last_q[:, SUBLANES - 3:])
        delta_p.append(s_fin)

    return (xp.reshape(n_seq, seq_len, D_MODEL), xs.reshape(n_dec, dec_len, D_MODEL),
            jnp.stack(conv_a_p), jnp.stack(conv_q_p), jnp.stack(delta_p),
            jnp.swapaxes(jnp.stack(conv_a_s), 1, 2), jnp.swapaxes(jnp.stack(conv_q_s), 1, 2), delta_s)
```

```python
import functools

import jax
import jax.numpy as jnp
from jax import lax
from jax.experimental import pallas as pl
from jax.experimental.pallas import tpu as pltpu

F32 = jnp.float32
BF16 = jnp.bfloat16

D_MODEL = 2048
D_CONV = 1024
D_DN = 1024
N_HEADS = 8
DK = 128
D_FF = 8192
D_MAIN = 3 * D_CONV + 4 * D_DN
QKV_OFF = 3 * D_CONV
Z_OFF = QKV_OFF + 3 * D_DN
EPS = 1e-6
LANES = 128
SUBLANES = 8
CHUNK = 128
SAMPLE_LEN = 4
VMEM_LIMIT = 56 * 1024 * 1024

PROMPT_TILES = dict(proj=(1024, 1792), out=(512, D_MODEL), ffn=(1024, 512))
SAMPLE_TILES = dict(proj=(512, 1024), out=(512, 1024), ffn=(512, 512))
ROW_SUB = 256
HIDDEN_SUB = 512
MIXER_ROWS = 256


def _dot(a, b):
    return jnp.dot(a, b, preferred_element_type=F32)


def _dot_nt(a, b):
    return lax.dot_general(a, b, (((1,), (1,)), ((), ())), preferred_element_type=F32)


def _dot_tn(a, b):
    return lax.dot_general(a, b, (((0,), (0,)), ((), ())), preferred_element_type=F32)


def _softplus(x):
    return jnp.maximum(x, 0.0) + jnp.log1p(jnp.exp(-jnp.abs(x)))


def _silu(x):
    return x * jax.nn.sigmoid(x)


def _rms(x):
    return x * lax.rsqrt(jnp.mean(x * x, axis=-1, keepdims=True) + EPS)


def _weight_spec(w, layer, block, index):
    if w.ndim == 2:
        return pl.BlockSpec(block, index, pipeline_mode=pl.Buffered(1) if block == w.shape else None)
    return pl.BlockSpec((None,) + block, lambda *g: (layer,) + index(*g))


def _bf16_weight_ref(w_ref, copy_ref):
    if copy_ref is None:
        return w_ref
    copy_ref[...] = w_ref[...].astype(BF16)
    return copy_ref


def _proj_kernel(x_ref, nw_ref, w_ref, wab_ref, alog_ref, dtb_ref, proj_ref, gb_ref, *rest, tm, emit):
    w16_ref, h_scr, wab_scr = rest if emit else (None,) + rest
    w = _bf16_weight_ref(w_ref, w16_ref)
    first = pl.program_id(1) == 0

    @pl.when(first)
    def _():
        wab_scr[...] = jnp.zeros_like(wab_scr)
        wab_scr[:2 * N_HEADS, :] = wab_ref[...].astype(BF16)
        for r in range(0, tm, ROW_SUB):
            rows = slice(r, r + ROW_SUB)
            h = (_rms(x_ref[rows, :]) * nw_ref[...]).astype(BF16)
            h_scr[rows, :] = h
            ab = _dot_nt(h, wab_scr[...])
            lane = lax.broadcasted_iota(jnp.int32, ab.shape, 1)
            g = -jnp.exp(alog_ref[...]) * _softplus(ab + dtb_ref[...])
            gb_ref[rows, :] = jnp.where(lane < N_HEADS, g, jax.nn.sigmoid(ab))
            proj_ref[rows, :] = _dot_nt(h, w[...])

    @pl.when(jnp.logical_not(first))
    def _():
        proj_ref[...] = _dot_nt(h_scr[...], w[...])


def _emits_bf16(w, m, tm):
    emit = w.dtype == F32
    assert not emit or m == tm
    return emit


def _proj(x, nw, w, w_in_t, alog, dtb, *, layer, tm, tn):
    m = x.shape[0]
    emit = _emits_bf16(w, m, tm)
    n_gate = 2 * N_HEADS
    out_shape = [jax.ShapeDtypeStruct((m, D_MAIN), F32), jax.ShapeDtypeStruct((m, LANES), F32)]
    out_specs = [pl.BlockSpec((tm, tn), lambda i, j: (i, j)), pl.BlockSpec((tm, LANES), lambda i, j: (i, 0))]
    if emit:
        out_shape.append(jax.ShapeDtypeStruct((D_MAIN, D_MODEL), BF16))
        out_specs.append(pl.BlockSpec((tn, D_MODEL), lambda i, j: (j, 0)))
    outs = pl.pallas_call(
        functools.partial(_proj_kernel, tm=tm, emit=emit),
        out_shape=tuple(out_shape),
        grid=(m // tm, D_MAIN // tn),
        in_specs=[
            pl.BlockSpec((tm, D_MODEL), lambda i, j: (i, 0)),
            pl.BlockSpec((1, D_MODEL), lambda i, j: (0, 0)),
            _weight_spec(w, layer, (tn, D_MODEL), lambda i, j: (j, 0)),
            pl.BlockSpec((None, n_gate, D_MODEL), lambda i, j: (layer, D_MAIN // n_gate, 0)),
            pl.BlockSpec((1, LANES), lambda i, j: (0, 0)),
            pl.BlockSpec((1, LANES), lambda i, j: (0, 0)),
        ],
        out_specs=tuple(out_specs),
        scratch_shapes=[pltpu.VMEM((tm, D_MODEL), BF16), pltpu.VMEM((LANES, D_MODEL), BF16)],
        compiler_params=pltpu.CompilerParams(
            dimension_semantics=("arbitrary", "arbitrary"), vmem_limit_bytes=VMEM_LIMIT),
        name="norm_proj",
    )(x, nw, w, w_in_t, alog, dtb)
    return outs if emit else (*outs, w)


def _outproj_kernel(y_ref, w_ref, x_ref, o_ref, w16_ref=None):
    o_ref[...] = x_ref[...] + _dot(y_ref[...], _bf16_weight_ref(w_ref, w16_ref)[...])


def _outproj(y, w, x, *, layer, tm, tn):
    m = x.shape[0]
    emit = _emits_bf16(w, m, tm)
    out_shape = [jax.ShapeDtypeStruct((m, D_MODEL), F32)]
    out_specs = [pl.BlockSpec((tm, tn), lambda i, j: (i, j))]
    if emit:
        out_shape.append(jax.ShapeDtypeStruct((D_MODEL, D_MODEL), BF16))
        out_specs.append(pl.BlockSpec((D_MODEL, tn), lambda i, j: (0, j)))
    outs = pl.pallas_call(
        _outproj_kernel,
        out_shape=tuple(out_shape),
        grid=(m // tm, D_MODEL // tn),
        in_specs=[
            pl.BlockSpec((tm, D_MODEL), lambda i, j: (i, 0)),
            _weight_spec(w, layer, (D_MODEL, tn), lambda i, j: (0, j)),
            pl.BlockSpec((tm, tn), lambda i, j: (i, j)),
        ],
        out_specs=tuple(out_specs),
        compiler_params=pltpu.CompilerParams(
            dimension_semantics=("arbitrary", "arbitrary"), vmem_limit_bytes=VMEM_LIMIT),
        name="out_proj",
    )(y, w, x)
    return outs if emit else (*outs, w)


def _ffn_kernel(x_ref, nw_ref, wu_ref, wd_ref, fw_ref, *rest, tm, final, emit, n_cast):
    cast_in, rest = rest[:n_cast], rest[n_cast:]
    o_ref, rest = rest[0], rest[1:]
    cast_out, rest = rest[len(rest) - 1 - n_cast:len(rest) - 1], rest[:len(rest) - 1 - n_cast] + rest[len(rest) - 1:]
    wu16_ref, wd16_ref, h_scr = rest if emit else (None, None) + rest
    f = pl.program_id(1)

    def cast_slabs():
        for src, dst in zip(cast_in, cast_out):
            dst[...] = src[...].astype(BF16)
    wu = _bf16_weight_ref(wu_ref, wu16_ref)
    wd = _bf16_weight_ref(wd_ref, wd16_ref)

    def add_mlp(h, rows, start):
        for c in range(0, wu.shape[1], HIDDEN_SUB):
            up = jnp.maximum(_dot(h, wu[:, c:c + HIDDEN_SUB]), 0.0)
            part = _dot((up * up).astype(BF16), wd[c:c + HIDDEN_SUB, :])
            if c == 0 and start is not None:
                o_ref[rows, :] = start + part
            else:
                o_ref[rows, :] += part

    @pl.when(f == 0)
    def _():
        cast_slabs()
        for r in range(0, tm, ROW_SUB):
            rows = slice(r, r + ROW_SUB)
            x = x_ref[rows, :]
            h = (_rms(x) * nw_ref[...]).astype(BF16)
            h_scr[rows, :] = h
            add_mlp(h, rows, x)

    @pl.when(f > 0)
    def _():
        cast_slabs()
        add_mlp(h_scr[...], slice(None), None)

    if final:
        @pl.when(f == pl.num_programs(1) - 1)
        def _():
            for r in range(0, tm, ROW_SUB):
                rows = slice(r, r + ROW_SUB)
                o_ref[rows, :] = _rms(o_ref[rows, :]) * fw_ref[...]


BF16_ROWS = 16


def _ffn(x, nw, wu, wd, fw, *, layer, tm, tf, final, cast=()):
    m = x.shape[0]
    emit = _emits_bf16(wu, m, tm)
    assert (wd.dtype == F32) == emit
    n_f = D_FF // tf
    steps = (m // tm) * n_f
    out_shape = [jax.ShapeDtypeStruct((m, D_MODEL), F32)]
    out_specs = [pl.BlockSpec((tm, D_MODEL), lambda i, f: (i, 0))]
    if emit:
        out_shape += [jax.ShapeDtypeStruct((D_MODEL, D_FF), BF16), jax.ShapeDtypeStruct((D_FF, D_MODEL), BF16)]
        out_specs += [pl.BlockSpec((D_MODEL, tf), lambda i, f: (0, f)), pl.BlockSpec((tf, D_MODEL), lambda i, f: (f, 0))]
    cast_specs = []
    for arr, rows in cast:
        per = -(-rows // steps)
        per = -(-per // BF16_ROWS) * BF16_ROWS
        assert rows % per == 0 and rows // per <= steps
        slab = lambda i, f, last=rows // per - 1: jnp.minimum(i * n_f + f, last)
        cast_specs.append(pl.BlockSpec((None, per, arr.shape[2]), lambda i, f, slab=slab: (layer + 1, slab(i, f), 0)))
        out_shape.append(jax.ShapeDtypeStruct((rows, arr.shape[2]), BF16))
        out_specs.append(pl.BlockSpec((per, arr.shape[2]), lambda i, f, slab=slab: (slab(i, f), 0)))
    outs = pl.pallas_call(
        functools.partial(_ffn_kernel, tm=tm, final=final, emit=emit, n_cast=len(cast)),
        out_shape=tuple(out_shape),
        grid=(m // tm, n_f),
        in_specs=[
            pl.BlockSpec((tm, D_MODEL), lambda i, f: (i, 0)),
            pl.BlockSpec((1, D_MODEL), lambda i, f: (0, 0)),
            _weight_spec(wu, layer, (D_MODEL, tf), lambda i, f: (0, f)),
            _weight_spec(wd, layer, (tf, D_MODEL), lambda i, f: (f, 0)),
            pl.BlockSpec((1, D_MODEL), lambda i, f: (0, 0)),
        ] + cast_specs,
        out_specs=tuple(out_specs),
        scratch_shapes=[pltpu.VMEM((tm, D_MODEL), BF16)],
        compiler_params=pltpu.CompilerParams(
            dimension_semantics=("arbitrary", "arbitrary"), vmem_limit_bytes=VMEM_LIMIT),
        name="ffn",
    )(x, nw, wu, wd, fw, *[arr for arr, _ in cast])
    if emit:
        return outs
    return (outs[0], wu, wd, *outs[1:])


def _group_masks(n, group):
    row = lax.broadcasted_iota(jnp.int32, (n, n), 0)
    col = lax.broadcasted_iota(jnp.int32, (n, n), 1)
    same = (row // group) == (col // group)
    incl = same & (row >= col)
    strict = same & (row > col)
    eye = jnp.where(row == col, 1.0, 0.0).astype(F32)
    return row, col, incl, strict, eye


def _merge_masks(row, col, base, group):
    masks = []
    b = base
    while b < group:
        masks.append(((row // (2 * b)) == (col // (2 * b))) & ((row // b) != (col // b)))
        b *= 2
    return masks


def _each(f, *lists):
    return [f(*args) for args in zip(*lists)]


def _unit_lower_inverse(a, eye, base, base_mask, merge_masks):
    n1 = a if base_mask is None else _each(lambda t: jnp.where(base_mask, t, 0.0), a)
    n1b = _each(lambda t: t.astype(BF16), n1)
    n2 = _each(_dot, n1b, n1b)
    n2b = _each(lambda t: t.astype(BF16), n2)
    n3 = _each(_dot, n1b, n2b)
    x = _each(lambda p1, p2, p3: eye - p1 + p2 - p3, n1, n2, n3)
    if base == 8:
        n4b = _each(lambda t: _dot(t, t).astype(BF16), n2b)
        x = _each(lambda t, p4: t + _dot(t.astype(BF16), p4), x, n4b)
    for m in merge_masks:
        xb = _each(lambda t: t.astype(BF16), x)
        xl = _each(lambda tb, t: _dot(tb, jnp.where(m, t, 0.0).astype(BF16)).astype(BF16), xb, a)
        x = _each(lambda t, l, tb: t - _dot(l, tb), x, xl, xb)
    return x


def _chunk_prepare(q, k, v, gcol, bcol, grow, incl, strict, inverse):
    decay = _each(lambda gc, gr: jnp.exp(jnp.where(incl, gc - gr, -jnp.inf)), gcol, grow)
    kb = _each(lambda t: t.astype(BF16), k)
    kk = _each(_dot_nt, kb, kb)
    qk = _each(lambda t, tb: _dot_nt(t.astype(BF16), tb), q, kb)
    a = _each(lambda b, d, m: jnp.where(strict, b * d * m, 0.0), bcol, decay, kk)
    t = inverse(a)
    eg = _each(jnp.exp, gcol)
    rhs = _each(lambda b, e, vv, kx: jnp.concatenate([b * vv, (b * e) * kx], axis=1).astype(BF16), bcol, eg, v, k)
    sol = _each(lambda tt, r: _dot(tt.astype(BF16), r), t, rhs)
    return (_each(lambda s: s[:, :DK], sol), _each(lambda s: s[:, DK:], sol),
            _each(lambda qq, e: qq * e, q, eg), _each(lambda m, d: m * d, qk, decay))


def _gated_out(o, z, dnw):
    return (_rms(o) * dnw * _silu(z)).astype(BF16)


def _conv_a_group(p_ref, gi, caw_ref, canw_ref, shift):
    cols = slice(gi * LANES, (gi + 1) * LANES)
    c_cols = slice(D_CONV + gi * LANES, D_CONV + (gi + 1) * LANES)
    h_cols = slice(2 * D_CONV + gi * LANES, 2 * D_CONV + (gi + 1) * LANES)
    b_a = p_ref[:, cols]
    u = p_ref[:, c_cols] * p_ref[:, h_cols]
    conv = (caw_ref[2:3, cols] * u + caw_ref[1:2, cols] * shift(u, 1, cols)
            + caw_ref[0:1, cols] * shift(u, 2, cols))
    y = b_a * conv
    return u, (_rms(y) * canw_ref[:, cols]).astype(BF16)


def _conv_qkv_group(p_ref, gi, cqw_ref, shift):
    cols = slice(gi * LANES, (gi + 1) * LANES)
    x_cols = slice(QKV_OFF + gi * LANES, QKV_OFF + (gi + 1) * LANES)
    x = p_ref[:, x_cols]
    conv = (cqw_ref[3:4, cols] * x + cqw_ref[2:3, cols] * shift(x, 1, cols)
            + cqw_ref[1:2, cols] * shift(x, 2, cols) + cqw_ref[0:1, cols] * shift(x, 3, cols))
    c = _silu(conv)
    if gi < 2 * N_HEADS:
        c = c * lax.rsqrt(jnp.sum(c * c, axis=-1, keepdims=True) + EPS)
        if gi < N_HEADS:
            c = c * (DK ** -0.5)
    return x, c


def _mixer_prompt_kernel(p_ref, gb_ref, caw_ref, canw_ref, cqw_ref, dnw_ref, x_ref, wo_ref,
                         xo_ref, lasta_ref, lastq_ref, sfin_ref,
                         y_ref, qkv_scr, ua_carry, xq_carry, st_scr, *, rb):
    nb = pl.program_id(1)

    @pl.when(nb == 0)
    def _():
        ua_carry[...] = jnp.zeros_like(ua_carry)
        xq_carry[...] = jnp.zeros_like(xq_carry)
        st_scr[...] = jnp.zeros_like(st_scr)

    row8 = lax.broadcasted_iota(jnp.int32, (SUBLANES, LANES), 0)

    def make_shift(carry_ref):
        def shift(x, s, cols):
            rolled = pltpu.roll(x, s, 0)
            prev = pltpu.roll(carry_ref[:, cols], s, 0)
            first = jnp.where(row8 < s, prev, rolled[:SUBLANES])
            return jnp.concatenate([first, rolled[SUBLANES:]], axis=0)
        return shift

    shift_a = make_shift(ua_carry)
    for gi in range(D_CONV // LANES):
        cols = slice(gi * LANES, (gi + 1) * LANES)
        u, y = _conv_a_group(p_ref, gi, caw_ref, canw_ref, shift_a)
        ua_carry[:, cols] = u[rb - SUBLANES:]
        y_ref[:, cols] = y

    shift_q = make_shift(xq_carry)
    for gi in range(3 * D_DN // LANES):
        cols = slice(gi * LANES, (gi + 1) * LANES)
        x, c = _conv_qkv_group(p_ref, gi, cqw_ref, shift_q)
        xq_carry[:, cols] = x[rb - SUBLANES:]
        qkv_scr[:, cols] = c

    row, col, incl, strict, eye = _group_masks(CHUNK, CHUNK)
    base_mask = (row // 8) == (col // 8)
    merges = _merge_masks(row, col, 8, CHUNK)
    inverse = functools.partial(_unit_lower_inverse, eye=eye, base=8, base_mask=base_mask, merge_masks=merges)
    rowc = lax.broadcasted_iota(jnp.int32, (CHUNK, LANES), 0)

    heads = list(range(N_HEADS))
    chunks = list(range(rb // CHUNK))
    q, k, v, gcol, bcol, grow = [], [], [], [], [], []
    for c in chunks:
        rows = slice(c * CHUNK, (c + 1) * CHUNK)
        gb = gb_ref[rows, :]
        gcum = gb
        s = 1
        while s < CHUNK:
            gcum = gcum + jnp.where(rowc >= s, pltpu.roll(gcum, s, 0), 0.0)
            s *= 2
        gcum_t = gcum.T
        q += [qkv_scr[rows, h * DK:(h + 1) * DK] for h in heads]
        k += [qkv_scr[rows, D_DN + h * DK:D_DN + (h + 1) * DK] for h in heads]
        v += [qkv_scr[rows, 2 * D_DN + h * DK:2 * D_DN + (h + 1) * DK] for h in heads]
        gcol += [gcum[:, h:h + 1] for h in heads]
        bcol += [gb[:, N_HEADS + h:N_HEADS + h + 1] for h in heads]
        grow += [gcum_t[h:h + 1, :] for h in heads]
    ub, w, qd, p = _chunk_prepare(q, k, v, gcol, bcol, grow, incl, strict, inverse)
    ke = _each(lambda a, g: (a * jnp.exp(g[CHUNK - 1:CHUNK, :] - g)).astype(BF16), k, gcol)
    lhs = _each(lambda a, b: jnp.concatenate([a, b], axis=0).astype(BF16), w, qd)
    p16 = _each(lambda a: a.astype(BF16), p)
    g_end = _each(lambda g: jnp.exp(g[CHUNK - 1:CHUNK, :]), gcol)

    st = [st_scr[h] for h in heads]
    for c in chunks:
        rows = slice(c * CHUNK, (c + 1) * CHUNK)
        sl = slice(c * N_HEADS, (c + 1) * N_HEADS)
        ws = _each(lambda a, s: _dot(a, s.astype(BF16)), lhs[sl], st)
        u16 = _each(lambda a, b: (a - b[:CHUNK]).astype(BF16), ub[sl], ws)
        o = _each(lambda a, b, c: a[CHUNK:] + _dot(b, c), ws, p16[sl], u16)
        st = _each(lambda g, s, a, b: g * s + _dot_tn(a, b), g_end[sl], st, ke[sl], u16)
        for h in heads:
            y_ref[rows, D_CONV + h * DK:D_CONV + (h + 1) * DK] = _gated_out(
                o[h], p_ref[rows, Z_OFF + h * DK:Z_OFF + (h + 1) * DK], dnw_ref[...])
    for h in heads:
        st_scr[h] = st[h]
    xo_ref[...] = x_ref[...] + _dot(y_ref[...], wo_ref[...])

    @pl.when(nb == pl.num_programs(1) - 1)
    def _():
        lasta_ref[0] = ua_carry[...]
        lastq_ref[0] = xq_carry[...]
        for h in range(N_HEADS):
            sfin_ref[0, h] = st_scr[h].T


def _mixer_prompt(proj, gb, caw, canw, cqw, dnw, x, w_out16, *, n_seq, seq_len, rb=MIXER_ROWS):
    nblk = seq_len // rb
    m = n_seq * seq_len
    full = lambda shape: pl.BlockSpec(shape, lambda b, n: (0,) * len(shape))
    return pl.pallas_call(
        functools.partial(_mixer_prompt_kernel, rb=rb),
        out_shape=(
            jax.ShapeDtypeStruct((m, D_MODEL), F32),
            jax.ShapeDtypeStruct((n_seq, SUBLANES, D_CONV), F32),
            jax.ShapeDtypeStruct((n_seq, SUBLANES, 3 * D_DN), F32),
            jax.ShapeDtypeStruct((n_seq, N_HEADS, DK, DK), F32),
        ),
        grid=(n_seq, nblk),
        in_specs=[
            pl.BlockSpec((rb, D_MAIN), lambda b, n: (b * nblk + n, 0)),
            pl.BlockSpec((rb, LANES), lambda b, n: (b * nblk + n, 0)),
            full((SUBLANES, D_CONV)),
            full((1, D_CONV)),
            full((SUBLANES, 3 * D_DN)),
            full((1, DK)),
            pl.BlockSpec((rb, D_MODEL), lambda b, n: (b * nblk + n, 0)),
            pl.BlockSpec((D_MODEL, D_MODEL), lambda b, n: (0, 0), pipeline_mode=pl.Buffered(1)),
        ],
        out_specs=(
            pl.BlockSpec((rb, D_MODEL), lambda b, n: (b * nblk + n, 0)),
            pl.BlockSpec((1, SUBLANES, D_CONV), lambda b, n: (b, 0, 0)),
            pl.BlockSpec((1, SUBLANES, 3 * D_DN), lambda b, n: (b, 0, 0)),
            pl.BlockSpec((1, N_HEADS, DK, DK), lambda b, n: (b, 0, 0, 0)),
        ),
        scratch_shapes=[
            pltpu.VMEM((rb, D_MODEL), BF16),
            pltpu.VMEM((rb, 3 * D_DN), F32),
            pltpu.VMEM((SUBLANES, D_CONV), F32),
            pltpu.VMEM((SUBLANES, 3 * D_DN), F32),
            pltpu.VMEM((N_HEADS, DK, DK), F32),
        ],
        compiler_params=pltpu.CompilerParams(
            dimension_semantics=("arbitrary", "arbitrary"), vmem_limit_bytes=VMEM_LIMIT),
        name="mixer_prompt",
    )(proj, gb, caw, canw, cqw, dnw, x, w_out16)


SEQ_PER_STEP = 8
ROWS_PER_STEP = SEQ_PER_STEP * SAMPLE_LEN


def _mixer_sample_kernel(p_ref, gb_ref, bufa_ref, bufq_ref, caw_ref, canw_ref, cqw_ref, dnw_ref, s_ref, prev_ref,
                         y_ref, newa_ref, newq_ref, snew_ref,
                         w_scr, qd_scr, ub_scr, ke_scr, p_scr, u_scr, qs_scr, g_scr, ua_scr, bxa_scr, xq_scr, bxq_scr):
    del prev_ref
    j = pl.program_id(1)
    n = CHUNK
    seqs = n // SAMPLE_LEN

    def token_rows(t):
        return pl.ds(t, seqs, stride=SAMPLE_LEN)

    @pl.when(j == 0)
    def _():
        rown = lax.broadcasted_iota(jnp.int32, (n, LANES), 0)
        tok = rown % SAMPLE_LEN

        def make_shift(buf_ref, bx_scr):
            first = SAMPLE_LEN - buf_ref.shape[0]
            bx_scr[...] = jnp.zeros_like(bx_scr)
            for m in range(buf_ref.shape[0]):
                for g in range(bx_scr.shape[0]):
                    bx_scr[g, token_rows(first + m), :] = buf_ref[m, :, g * LANES:(g + 1) * LANES]

            def shift(x, s, cols):
                return jnp.where(tok >= s, pltpu.roll(x, s, 0),
                                 pltpu.roll(bx_scr[cols.start // LANES], n - (SAMPLE_LEN - s), 0))
            return shift

        def emit_tail(raw_scr, new_ref):
            first = SAMPLE_LEN - new_ref.shape[0]
            for m in range(new_ref.shape[0]):
                for g in range(raw_scr.shape[0]):
                    new_ref[m, :, g * LANES:(g + 1) * LANES] = raw_scr[g, token_rows(first + m), :]

        shift_a = make_shift(bufa_ref, bxa_scr)
        for gi in range(D_CONV // LANES):
            cols = slice(gi * LANES, (gi + 1) * LANES)
            u, y = _conv_a_group(p_ref, gi, caw_ref, canw_ref, shift_a)
            ua_scr[gi] = u
            y_ref[:, cols] = y
        emit_tail(ua_scr, newa_ref)

        shift_q = make_shift(bufq_ref, bxq_scr)
        qkv = [None] * (3 * N_HEADS)
        for gi in range(3 * D_DN // LANES):
            xq_scr[gi], qkv[gi] = _conv_qkv_group(p_ref, gi, cqw_ref, shift_q)
        emit_tail(xq_scr, newq_ref)

        row, col, incl, strict, eye = _group_masks(n, SAMPLE_LEN)
        inverse = functools.partial(_unit_lower_inverse, eye=eye, base=SAMPLE_LEN, base_mask=None, merge_masks=[])

        gb = gb_ref[...]
        gcum = gb
        s = 1
        while s < SAMPLE_LEN:
            gcum = gcum + jnp.where(tok >= s, pltpu.roll(gcum, s, 0), 0.0)
            s *= 2
        glast = gcum
        for back in range(1, SAMPLE_LEN):
            glast = jnp.where(tok == SAMPLE_LEN - 1 - back, pltpu.roll(gcum, n - back, 0), glast)
        g_scr[...] = gcum
        gcum_t = gcum.T
        e_end = jnp.exp(glast - gcum)
        heads = list(range(N_HEADS))
        gcol = [gcum[:, h:h + 1] for h in heads]
        bcol = [gb[:, N_HEADS + h:N_HEADS + h + 1] for h in heads]
        grow = [gcum_t[h:h + 1, :] for h in heads]
        ub, w, qd, p = _chunk_prepare(qkv[:N_HEADS], qkv[N_HEADS:2 * N_HEADS], qkv[2 * N_HEADS:],
                                      gcol, bcol, grow, incl, strict, inverse)
        for h in heads:
            hc = slice(h * DK, (h + 1) * DK)
            ub_scr[:, hc] = ub[h]
            w_scr[:, hc] = w[h]
            qd_scr[:, hc] = qd[h]
            ke_scr[:, hc] = qkv[N_HEADS + h] * e_end[:, h:h + 1]
            p_scr[h] = p[h]
        u_scr[...] = jnp.zeros_like(u_scr)

    r0 = pl.multiple_of(j * ROWS_PER_STEP, ROWS_PER_STEP)
    row16 = lax.broadcasted_iota(jnp.int32, (2 * SUBLANES, LANES), 0)
    first_of_pair = (row16 % SUBLANES) < SAMPLE_LEN
    coln = lax.broadcasted_iota(jnp.int32, (DK, n), 1)
    heads = list(range(N_HEADS))
    hcs = [slice(h * DK, (h + 1) * DK) for h in heads]
    for h in heads:
        for t in range(ROWS_PER_STEP // SUBLANES):
            rows = pl.ds(pl.multiple_of(r0 + t * SUBLANES, SUBLANES), SUBLANES)
            lhs = jnp.concatenate([w_scr[rows, hcs[h]], qd_scr[rows, hcs[h]]], axis=0).astype(BF16)
            da = _dot_nt(lhs, s_ref[2 * t, h].astype(BF16))
            db = _dot_nt(lhs, s_ref[2 * t + 1, h].astype(BF16))
            ws = jnp.where(first_of_pair, da, db)
            u_scr[rows, hcs[h]] = ub_scr[rows, hcs[h]] - ws[:SUBLANES]
            qs_scr[rows, hcs[h]] = ws[SUBLANES:]
    u_h = [u_scr[:, hc] for hc in hcs]
    u_t = _each(lambda t: t.T.astype(BF16), u_h)
    ke16 = [ke_scr[:, hc].astype(BF16) for hc in hcs]
    for sq in range(SEQ_PER_STEP):
        seq_cols = jnp.where(coln // SAMPLE_LEN == j * SEQ_PER_STEP + sq, 1.0, 0.0).astype(BF16)
        g_end = jnp.exp(g_scr[pl.ds(r0 + sq * SAMPLE_LEN + SAMPLE_LEN - 1, 1), :])
        upd = _each(lambda t, ke: _dot(t * seq_cols, ke), u_t, ke16)
        for h in heads:
            snew_ref[sq, h] = g_end[:, h:h + 1] * s_ref[sq, h] + upd[h]
    rows = pl.ds(r0, ROWS_PER_STEP)
    o = [qs_scr[rows, hcs[h]] + _dot(p_scr[h, rows, :].astype(BF16), u_h[h].astype(BF16)) for h in heads]
    for h in heads:
        y_ref[rows, D_CONV + h * DK:D_CONV + (h + 1) * DK] = _gated_out(
            o[h], p_ref[rows, Z_OFF + h * DK:Z_OFF + (h + 1) * DK], dnw_ref[...])


def _mixer_sample(proj, gb, bufa, bufq, caw, canw, cqw, dnw, state, new_state, *, layer):
    m = proj.shape[0]
    n = CHUNK
    seqs = n // SAMPLE_LEN
    n_seq = m // SAMPLE_LEN
    steps = n // ROWS_PER_STEP
    full = lambda shape: pl.BlockSpec(shape, lambda i, j: (0,) * len(shape))
    rowblk = lambda width: pl.BlockSpec((n, width), lambda i, j: (i, 0))
    sblk = pl.BlockSpec((None, SEQ_PER_STEP, N_HEADS, DK, DK), lambda i, j: (layer, i * steps + j, 0, 0, 0))
    la, lq = bufa.shape[1], bufq.shape[1]
    prev_index = 9
    return pl.pallas_call(
        _mixer_sample_kernel,
        out_shape=(
            jax.ShapeDtypeStruct((m, D_MODEL), BF16),
            jax.ShapeDtypeStruct((la, n_seq, D_CONV), F32),
            jax.ShapeDtypeStruct((lq, n_seq, 3 * D_DN), F32),
            jax.ShapeDtypeStruct(state.shape, F32),
        ),
        grid=(m // n, steps),
        in_specs=[
            rowblk(D_MAIN), rowblk(LANES),
            pl.BlockSpec((None, la, seqs, D_CONV), lambda i, j: (layer, 0, i, 0)),
            pl.BlockSpec((None, lq, seqs, 3 * D_DN), lambda i, j: (layer, 0, i, 0)),
            full((SUBLANES, D_CONV)), full((1, D_CONV)), full((SUBLANES, 3 * D_DN)), full((1, DK)),
            sblk, pl.BlockSpec(memory_space=pl.ANY),
        ],
        input_output_aliases={prev_index: 3} if layer > 0 else {},
        out_specs=(rowblk(D_MODEL),
                   pl.BlockSpec((la, seqs, D_CONV), lambda i, j: (0, i, 0)),
                   pl.BlockSpec((lq, seqs, 3 * D_DN), lambda i, j: (0, i, 0)),
                   sblk),
        scratch_shapes=[
            pltpu.VMEM((n, D_DN), F32),
            pltpu.VMEM((n, D_DN), F32),
            pltpu.VMEM((n, D_DN), F32),
            pltpu.VMEM((n, D_DN), F32),
            pltpu.VMEM((N_HEADS, n, n), F32),
            pltpu.VMEM((n, D_DN), F32),
            pltpu.VMEM((n, D_DN), F32),
            pltpu.VMEM((n, LANES), F32),
            pltpu.VMEM((D_CONV // LANES, n, LANES), F32),
            pltpu.VMEM((D_CONV // LANES, n, LANES), F32),
            pltpu.VMEM((3 * D_DN // LANES, n, LANES), F32),
            pltpu.VMEM((3 * D_DN // LANES, n, LANES), F32),
        ],
        compiler_params=pltpu.CompilerParams(
            dimension_semantics=("arbitrary", "arbitrary"), vmem_limit_bytes=VMEM_LIMIT),
        name="mixer_sample",
    )(proj, gb, bufa, bufq, caw, canw, cqw, dnw, state, new_state)


def _pad_rows(w):
    return jnp.pad(w, ((0, 0), (0, SUBLANES - w.shape[1]), (0, 0)))


def kernel(x_prompt, x_sample, state_conv_a, state_conv_qkv, state_delta, norm_mix_w, w_in,
           conv_a_w, conv_a_norm_w, conv_qkv_w, a_log, dt_bias, dn_norm_w, w_out,
           norm_ffn_w, w_up, w_down, final_norm_w):
    depth = w_in.shape[0]
    n_seq, seq_len, _ = x_prompt.shape
    n_dec, dec_len, _ = x_sample.shape
    assert dec_len == SAMPLE_LEN

    w_in_t = jnp.swapaxes(w_in, 1, 2)
    alog = jnp.pad(a_log, ((0, 0), (0, LANES - N_HEADS)))[:, None, :]
    dtb = jnp.pad(dt_bias, ((0, 0), (0, LANES - N_HEADS)))[:, None, :]
    caw, cqw = _pad_rows(conv_a_w), _pad_rows(conv_qkv_w)
    bufa = jnp.swapaxes(state_conv_a, 1, 2)
    bufq = jnp.swapaxes(state_conv_qkv, 1, 2)

    xp = x_prompt.reshape(n_seq * seq_len, D_MODEL)
    xs = x_sample.reshape(n_dec * dec_len, D_MODEL)
    tp, ts = PROMPT_TILES, SAMPLE_TILES
    fw = final_norm_w[None, :]
    conv_a_p, conv_q_p, delta_p, conv_a_s, conv_q_s = [], [], [], [], []
    delta_s = state_delta
    for l in range(depth):
        nmw, nfw = norm_mix_w[l][None, :], norm_ffn_w[l][None, :]
        canw, dnw = conv_a_norm_w[l][None, :], dn_norm_w[l][None, :]
        final = l == depth - 1
        w_main, w_o, w_u, w_d = (w_in_t, w_out, w_up, w_down) if l == 0 else next16
        proj, gb, w_main16 = _proj(xs, nmw, w_main, w_in_t, alog[l], dtb[l], layer=l,
                                   tm=ts["proj"][0], tn=ts["proj"][1])
        y, new_a, new_q, delta_s = _mixer_sample(proj, gb, bufa, bufq, caw[l], canw, cqw[l], dnw, state_delta,
                                                 delta_s, layer=l)
        xs, w_out16 = _outproj(y, w_o, xs, layer=l, tm=ts["out"][0], tn=ts["out"][1])
        xs, w_up16, w_down16 = _ffn(xs, nfw, w_u, w_d, fw, layer=l, tm=ts["ffn"][0], tf=ts["ffn"][1],
                                    final=final)
        conv_a_s.append(new_a)
        conv_q_s.append(new_q)
        proj, gb, _ = _proj(xp, nmw, w_main16, w_in_t, alog[l], dtb[l], layer=l, tm=tp["proj"][0], tn=tp["proj"][1])
        xp, last_a, last_q, s_fin = _mixer_prompt(proj, gb, caw[l], canw, cqw[l], dnw, xp, w_out16,
                                                  n_seq=n_seq, seq_len=seq_len)
        cast = () if final else ((w_in_t, D_MAIN), (w_out, D_MODEL), (w_up, D_MODEL), (w_down, D_FF))
        xp, _, _, *next16 = _ffn(xp, nfw, w_up16, w_down16, fw, layer=l, tm=tp["ffn"][0], tf=tp["ffn"][1],
                                 final=final, cast=cast)
        conv_a_p.append(last_a[:, SUBLANES - 2:])
        conv_q_p.append(last_q[:, SUBLANES - 3:])
        delta_p.append(s_fin)

    return (xp.reshape(n_seq, seq_len, D_MODEL), xs.reshape(n_dec, dec_len, D_MODEL),
            jnp.stack(conv_a_p), jnp.stack(conv_q_p), jnp.stack(delta_p),
            jnp.swapaxes(jnp.stack(conv_a_s), 1, 2), jnp.swapaxes(jnp.stack(conv_q_s), 1, 2), delta_s)
```
